```python
import jax, jax.numpy as jnp
from jax import lax
import numpy as np

D_MODEL = 1024
BATCH = 8
SEQ = 4096
DEPTH = 4

CHUNK = 64
N_MIXERS = 2
EPS = 1e-6
D_FF = 256 * ((8 * D_MODEL // 3 + 255) // 256)
M_HEADS = 8
M_QK = D_MODEL // 2
M_V = D_MODEL
M_DK = M_QK // M_HEADS
M_DV = M_V // M_HEADS
M_IN = 2 * M_QK + 2 * M_V + 2 * M_HEADS
R_WIDTH = 128 * ((4 * D_MODEL // 3) // 128)
R_BLOCKS = 8
R_BW = R_WIDTH // R_BLOCKS
CONV_W = 4
RG_C = 8.0

N_M_LAYERS = (DEPTH + 1) // 2
N_R_LAYERS = DEPTH // 2

kernel_name = "hybrid_mlstm_rglru_macaron"


def rmsnorm(x, g):
    xf = x.astype(jnp.float32)
    y = xf * lax.rsqrt(jnp.mean(xf * xf, axis=-1, keepdims=True) + EPS)
    return (y * g.astype(jnp.float32)).astype(x.dtype)


def swiglu(x, w_in, w_out):
    gate, up = jnp.split(x @ w_in, 2, axis=-1)
    return (jax.nn.silu(gate) * up) @ w_out


def mlstm_chunkwise(q, k, v, ig, lf):
    B, S, H, DK = q.shape
    DV = v.shape[-1]
    NC = S // CHUNK

    def to_chunks(t):
        t = t.reshape((B, NC, CHUNK, H) + t.shape[3:])
        return jnp.moveaxis(t, (1, 3), (0, 2))

    causal = jnp.tril(jnp.ones((CHUNK, CHUNK), dtype=bool))

    def step(carry, inp):
        C, n, m = carry
        qc, kc, vc, ic, fc = inp
        b = jnp.cumsum(fc, axis=-1)
        dmat = b[..., :, None] - b[..., None, :] + ic[..., None, :]
        dmat = jnp.where(causal, dmat, -jnp.inf)
        m_inter = b + m[..., None]
        m_t = jnp.maximum(m_inter, jnp.max(dmat, axis=-1))
        s = jnp.einsum('bhtk,bhsk->bhts', qc, kc) * jnp.exp(dmat - m_t[..., None])
        w_inter = jnp.exp(m_inter - m_t)
        num = (jnp.einsum('bhts,bhsv->bhtv', s, vc)
               + w_inter[..., None] * jnp.einsum('bhtk,bhkv->bhtv', qc, C))
        den = jnp.sum(s, axis=-1) + w_inter * jnp.einsum('bhtk,bhk->bht', qc, n)
        h = num / jnp.maximum(jnp.abs(den), jnp.exp(-m_t))[..., None]
        b_last = b[..., -1]
        g = b_last[..., None] - b + ic
        m_new = jnp.maximum(b_last + m, jnp.max(g, axis=-1))
        decay = jnp.exp(b_last + m - m_new)
        wk = jnp.exp(g - m_new[..., None])
        C_new = decay[..., None, None] * C + jnp.einsum('bhs,bhsk,bhsv->bhkv', wk, kc, vc)
        n_new = decay[..., None] * n + jnp.einsum('bhs,bhsk->bhk', wk, kc)
        return (C_new, n_new, m_new), h

    init = (jnp.zeros((B, H, DK, DV), jnp.float32),
            jnp.zeros((B, H, DK), jnp.float32),
            jnp.zeros((B, H), jnp.float32))
    xs = (to_chunks(q), to_chunks(k), to_chunks(v), to_chunks(ig), to_chunks(lf))
    _, h = lax.scan(step, init, xs)
    h = jnp.moveaxis(h, (0, 2), (1, 3))
    return h.reshape(B, S, H, DV)


def mlstm_mixer(x, w_in, b_i, b_f, head_norm, w_out):
    B, S, _ = x.shape
    z = x @ w_in
    q, k, v, o, ig, fg = jnp.split(
        z, [M_QK, 2 * M_QK, 2 * M_QK + M_V, 2 * M_QK + 2 * M_V, 2 * M_QK + 2 * M_V + M_HEADS],
        axis=-1)
    q = q.astype(jnp.float32).reshape(B, S, M_HEADS, M_DK) * (M_DK ** -0.5)
    k = k.astype(jnp.float32).reshape(B, S, M_HEADS, M_DK)
    v = v.astype(jnp.float32).reshape(B, S, M_HEADS, M_DV)
    ig = ig.astype(jnp.float32) + b_i.astype(jnp.float32)
    lf = jax.nn.log_sigmoid(fg.astype(jnp.float32) + b_f.astype(jnp.float32))
    h = mlstm_chunkwise(q, k, v, ig, lf)
    h = h * lax.rsqrt(jnp.mean(h * h, axis=-1, keepdims=True) + EPS)
    h = h.reshape(B, S, M_V) * head_norm.astype(jnp.float32)
    y = jax.nn.sigmoid(o) * h.astype(x.dtype)
    return y @ w_out


def _lin_combine(l, r):
    a_l, b_l = l
    a_r, b_r = r
    return a_l * a_r, a_r * b_l + b_r


def rglru_mixer(x, w_in, conv_w, conv_b, w_a, b_a, w_i, b_i, lam, w_out):
    B, S, _ = x.shape
    u, gate = jnp.split(x @ w_in, 2, axis=-1)
    u = lax.conv_general_dilated(u, conv_w, window_strides=(1,), padding=[(CONV_W - 1, 0)],
                                 dimension_numbers=('NWC', 'WIO', 'NWC'),
                                 feature_group_count=R_WIDTH) + conv_b
    ub = u.reshape(B, S, R_BLOCKS, R_BW)
    r = jax.nn.sigmoid(jnp.einsum('bsnc,ncd->bsnd', ub, w_a).reshape(B, S, R_WIDTH) + b_a)
    i = jax.nn.sigmoid(jnp.einsum('bsnc,ncd->bsnd', ub, w_i).reshape(B, S, R_WIDTH) + b_i)
    log_a = -RG_C * r.astype(jnp.float32) * jax.nn.softplus(-lam.astype(jnp.float32))
    a = jnp.exp(log_a)
    bterm = jnp.sqrt(-jnp.expm1(2.0 * log_a)) * (i * u).astype(jnp.float32)
    _, h = lax.associative_scan(_lin_combine, (a, bterm), axis=1)
    y = h.astype(x.dtype) * jax.nn.gelu(gate)
    return y @ w_out


def setup_inputs(seed: int = 0) -> dict:
    key = jax.random.key(seed)
    ks = jax.random.split(key, 32)
    f32 = jnp.float32

    def nrm(k, shape, scale):
        return jax.random.normal(k, shape, f32) * scale

    def gain(k, shape):
        return 1.0 + 0.1 * jax.random.normal(k, shape, f32)

    p_a = jax.random.uniform(ks[20], (N_R_LAYERS, R_WIDTH), f32, 0.9, 0.999) ** (1.0 / RG_C)
    return {
        "x": nrm(ks[0], (BATCH, SEQ, D_MODEL), 1.0),
        "ff1_norm": gain(ks[1], (DEPTH, D_MODEL)),
        "ff1_w_in": nrm(ks[2], (DEPTH, D_MODEL, 2 * D_FF), D_MODEL ** -0.5),
        "ff1_w_out": nrm(ks[3], (DEPTH, D_FF, D_MODEL), D_FF ** -0.5),
        "mix_norm": gain(ks[4], (DEPTH, D_MODEL)),
        "ff2_norm": gain(ks[5], (DEPTH, D_MODEL)),
        "ff2_w_in": nrm(ks[6], (DEPTH, D_MODEL, 2 * D_FF), D_MODEL ** -0.5),
        "ff2_w_out": nrm(ks[7], (DEPTH, D_FF, D_MODEL), D_FF ** -0.5),
        "m_w_in": nrm(ks[8], (N_M_LAYERS, D_MODEL, M_IN), D_MODEL ** -0.5),
        "m_b_i": nrm(ks[9], (N_M_LAYERS, M_HEADS), 0.1),
        "m_b_f": jnp.linspace(3.0, 6.0, M_HEADS, dtype=f32)[None, :] + nrm(ks[10], (N_M_LAYERS, M_HEADS), 0.1),
        "m_head_norm": gain(ks[11], (N_M_LAYERS, M_V)),
        "m_w_out": nrm(ks[12], (N_M_LAYERS, M_V, D_MODEL), M_V ** -0.5),
        "r_w_in": nrm(ks[13], (N_R_LAYERS, D_MODEL, 2 * R_WIDTH), D_MODEL ** -0.5),
        "r_conv_w": nrm(ks[14], (N_R_LAYERS, CONV_W, 1, R_WIDTH), CONV_W ** -0.5),
        "r_conv_b": nrm(ks[15], (N_R_LAYERS, R_WIDTH), 0.01),
        "r_w_a": nrm(ks[16], (N_R_LAYERS, R_BLOCKS, R_BW, R_BW), R_BW ** -0.5),
        "r_b_a": nrm(ks[17], (N_R_LAYERS, R_WIDTH), 0.01),
        "r_w_i": nrm(ks[18], (N_R_LAYERS, R_BLOCKS, R_BW, R_BW), R_BW ** -0.5),
        "r_b_i": nrm(ks[19], (N_R_LAYERS, R_WIDTH), 0.01),
        "r_lam": jnp.log(p_a) - jnp.log1p(-p_a),
        "r_w_out": nrm(ks[21], (N_R_LAYERS, R_WIDTH, D_MODEL), R_WIDTH ** -0.5),
        "final_norm": gain(ks[22], (D_MODEL,)),
    }


def reference(x, ff1_norm, ff1_w_in, ff1_w_out, mix_norm, ff2_norm, ff2_w_in, ff2_w_out,
              m_w_in, m_b_i, m_b_f, m_head_norm, m_w_out,
              r_w_in, r_conv_w, r_conv_b, r_w_a, r_b_a, r_w_i, r_b_i, r_lam, r_w_out,
              final_norm):
    for layer in range(DEPTH):
        x = x + 0.5 * swiglu(rmsnorm(x, ff1_norm[layer]), ff1_w_in[layer], ff1_w_out[layer])
        h = rmsnorm(x, mix_norm[layer])
        j = layer // N_MIXERS
        if layer % N_MIXERS == 0:
            h = mlstm_mixer(h, m_w_in[j], m_b_i[j], m_b_f[j], m_head_norm[j], m_w_out[j])
        else:
            h = rglru_mixer(h, r_w_in[j], r_conv_w[j], r_conv_b[j], r_w_a[j], r_b_a[j],
                            r_w_i[j], r_b_i[j], r_lam[j], r_w_out[j])
        x = x + h
        x = x + 0.5 * swiglu(rmsnorm(x, ff2_norm[layer]), ff2_w_in[layer], ff2_w_out[layer])
    return rmsnorm(x, final_norm)
```

```python
import functools

import jax
import jax.numpy as jnp
from jax import lax
from jax.experimental import pallas as pl
from jax.experimental.pallas import tpu as pltpu

D_MODEL = 1024
EPS = 1e-6
D_FF = 2816
M_HEADS = 8
M_QK = 512
M_V = 1024
M_DK = 64
M_DV = 128
R_WIDTH = 1280
R_BLOCKS = 8
R_BW = 160
CONV_W = 4
RG_C = 8.0

LANES = 128
SUBLANES = 8
VMEM_LIMIT = 56 * 1024 * 1024

FFN_TM = 512
FFN_TF = 256
M_TM = 256
R_TM = 256
R_SB = 640
GATE_PAD = 128

BF16 = jnp.bfloat16
F32 = jnp.float32


def _rms_scale(x, g):
    ms = jnp.mean(x * x, axis=-1, keepdims=True)
    return x * lax.rsqrt(ms + EPS) * g


def _dot(a, b):
    return jnp.dot(a, b, preferred_element_type=F32)


def _dot_nt(a, b):
    return lax.dot_general(a, b, (((1,), (1,)), ((), ())), preferred_element_type=F32)


def _softplus(x):
    return jnp.maximum(x, 0.0) + jnp.log1p(jnp.exp(-jnp.abs(x)))


def _log_sigmoid(x):
    return jnp.minimum(x, 0.0) - jnp.log1p(jnp.exp(-jnp.abs(x)))


def _ffn_kernel(x_ref, g_ref, win_ref, wout_ref, fg_ref, o_ref, act_ref, *, final_norm):
    x = x_ref[...]
    xn = _rms_scale(x, g_ref[...]).astype(BF16)
    for c in range(D_FF // FFN_TF):
        lo = c * FFN_TF
        hg = _dot(xn, win_ref[:, lo:lo + FFN_TF])
        hu = _dot(xn, win_ref[:, D_FF + lo:D_FF + lo + FFN_TF])
        act_ref[:, lo:lo + FFN_TF] = (hg * jax.nn.sigmoid(hg) * hu).astype(BF16)
    y = x + 0.5 * _dot(act_ref[...], wout_ref[...])
    if final_norm:
        y = _rms_scale(y, fg_ref[...])
    o_ref[...] = y


def _ffn(x2d, g, w_in, w_out, final_g, final_norm):
    T = x2d.shape[0]
    const = lambda i: (0, 0)
    return pl.pallas_call(
        functools.partial(_ffn_kernel, final_norm=final_norm),
        grid=(T // FFN_TM,),
        in_specs=[
            pl.BlockSpec((FFN_TM, D_MODEL), lambda i: (i, 0)),
            pl.BlockSpec((1, D_MODEL), const),
            pl.BlockSpec((D_MODEL, 2 * D_FF), const),
            pl.BlockSpec((D_FF, D_MODEL), const),
            pl.BlockSpec((1, D_MODEL), const),
        ],
        out_specs=pl.BlockSpec((FFN_TM, D_MODEL), lambda i: (i, 0)),
        out_shape=jax.ShapeDtypeStruct((T, D_MODEL), F32),
        scratch_shapes=[pltpu.VMEM((FFN_TM, D_FF), BF16)],
        compiler_params=pltpu.CompilerParams(
            dimension_semantics=("parallel",), vmem_limit_bytes=VMEM_LIMIT),
        name="ffn",
    )(x2d, g, w_in, w_out, final_g)


def _mlstm_kernel(x_ref, g_ref, w_ref, wkt_ref, wg_ref, wgt_ref, brow_ref, bcol_ref,
                  hn_ref, wout_ref, o_ref, c_ref, n_ref, m_ref, y_ref):
    L = M_TM

    @pl.when(pl.program_id(1) == 0)
    def _():
        c_ref[...] = jnp.zeros_like(c_ref)
        n_ref[...] = jnp.zeros_like(n_ref)
        m_ref[...] = jnp.zeros_like(m_ref)

    x = x_ref[0]
    xn = _rms_scale(x, g_ref[...]).astype(BF16)

    qk = _dot(xn, w_ref[:, 0:2 * M_QK])
    q = qk[:, 0:M_QK] * (M_DK ** -0.5)
    k = qk[:, M_QK:2 * M_QK]
    v = _dot(xn, w_ref[:, 2 * M_QK:2 * M_QK + M_V]).astype(BF16)
    og = _dot(xn, w_ref[:, 2 * M_QK + M_V:2 * M_QK + 2 * M_V])
    kt = _dot_nt(wkt_ref[...], xn)
    q_bf = q.astype(BF16)
    k_bf = k.astype(BF16)

    gc = _dot(xn, wg_ref[...]) + brow_ref[...]
    gr = _dot_nt(wgt_ref[...], xn) + bcol_ref[...]
    lf_c = _log_sigmoid(gc)
    lf_r = _log_sigmoid(gr)

    row = lax.broadcasted_iota(jnp.int32, (L, L), 0)
    col = lax.broadcasted_iota(jnp.int32, (L, L), 1)
    causal = col <= row
    tri = causal.astype(F32)
    b_c = jnp.dot(tri, lf_c, preferred_element_type=F32, precision=lax.Precision.HIGHEST)
    b_r = lax.dot_general(lf_r, tri, (((1,), (1,)), ((), ())),
                          preferred_element_type=F32, precision=lax.Precision.HIGHEST)

    lane = lax.broadcasted_iota(jnp.int32, (1, 2 * M_DK), 1)
    qn = q * n_ref[...]
    hnorm = hn_ref[...]

    for p in range(M_HEADS // 2):
        pair = slice(2 * M_DK * p, 2 * M_DK * (p + 1))
        q_pair = q_bf[:, pair]
        k_pair = k_bf[:, pair]
        k_pair32 = k[:, pair]
        qn_pair = qn[:, pair]
        c_pair = c_ref[pair, :].astype(BF16)
        n_pair = n_ref[:, pair]
        n_new = n_pair
        for j in range(2):
            h = 2 * p + j
            in_head = (lane >= j * M_DK) & (lane < (j + 1) * M_DK)
            hs = slice(M_DK * h, M_DK * (h + 1))
            vs = slice(M_DV * h, M_DV * (h + 1))
            fcol = M_HEADS + h

            bt = b_c[:, fcol:fcol + 1]
            ig_c = gc[:, h:h + 1]
            b_row = b_r[fcol:fcol + 1, :]
            ig_row = gr[h:h + 1, :]
            b_last = bt[L - 1:L, :]
            m_prev = m_ref[h:h + 1, 0:1]

            dm = jnp.where(causal, bt + (ig_row - b_row), -jnp.inf)
            m_inter = bt + m_prev
            m_t = jnp.maximum(m_inter, jnp.max(dm, axis=1, keepdims=True))
            dmat = jnp.exp(dm - m_t)
            q_m = jnp.where(in_head, q_pair, jnp.zeros_like(q_pair))
            s = _dot_nt(q_m, k_pair) * dmat
            w_inter = jnp.exp(m_inter - m_t)
            v_h = v[:, vs]
            num = _dot(s.astype(BF16), v_h) + w_inter * _dot(q_m, c_pair)
            qn_h = jnp.sum(jnp.where(in_head, qn_pair, 0.0), axis=1, keepdims=True)
            den = jnp.sum(s, axis=1, keepdims=True) + w_inter * qn_h
            hh = num * (1.0 / jnp.maximum(jnp.abs(den), jnp.exp(-m_t)))
            hh = hh * lax.rsqrt(jnp.mean(hh * hh, axis=1, keepdims=True) + EPS)
            hh = hh * hnorm[:, vs]
            y_ref[:, vs] = (jax.nn.sigmoid(og[:, vs]) * hh).astype(BF16)

            g_c = b_last - bt + ig_c
            g_r = b_last - b_row + ig_row
            m_new = jnp.maximum(b_last + m_prev, jnp.max(g_r, axis=1, keepdims=True))
            decay = jnp.exp(b_last + m_prev - m_new)
            wk_r = jnp.exp(g_r - m_new)
            wk_c = jnp.exp(g_c - m_new)
            kw = (kt[hs, :] * wk_r).astype(BF16)
            c_ref[hs, :] = decay * c_ref[hs, :] + _dot(kw, v_h)
            ksum = jnp.sum(k_pair32 * wk_c, axis=0, keepdims=True)
            n_new = jnp.where(in_head, decay * n_pair + ksum, n_new)
            m_ref[h:h + 1, :] = jnp.broadcast_to(m_new, (1, LANES))
        n_ref[:, pair] = n_new

    o_ref[0] = x + _dot(y_ref[...], wout_ref[...])


def _mlstm(x, g, w_main, wkt, wg, wgt, brow, bcol, hnorm, w_out):
    B, S, _ = x.shape
    const = lambda b, s: (0, 0)
    return pl.pallas_call(
        _mlstm_kernel,
        grid=(B, S // M_TM),
        in_specs=[
            pl.BlockSpec((1, M_TM, D_MODEL), lambda b, s: (b, s, 0)),
            pl.BlockSpec((1, D_MODEL), const),
            pl.BlockSpec((D_MODEL, 2 * M_QK + 2 * M_V), const),
            pl.BlockSpec((M_QK, D_MODEL), const),
            pl.BlockSpec((D_MODEL, GATE_PAD), const),
            pl.BlockSpec((2 * M_HEADS, D_MODEL), const),
            pl.BlockSpec((1, GATE_PAD), const),
            pl.BlockSpec((2 * M_HEADS, 1), const),
            pl.BlockSpec((1, M_V), const),
            pl.BlockSpec((M_V, D_MODEL), const),
        ],
        out_specs=pl.BlockSpec((1, M_TM, D_MODEL), lambda b, s: (b, s, 0)),
        out_shape=jax.ShapeDtypeStruct((B, S, D_MODEL), F32),
        scratch_shapes=[
            pltpu.VMEM((M_HEADS * M_DK, M_DV), F32),
            pltpu.VMEM((1, M_QK), F32),
            pltpu.VMEM((M_HEADS, LANES), F32),
            pltpu.VMEM((M_TM, M_V), BF16),
        ],
        compiler_params=pltpu.CompilerParams(
            dimension_semantics=("parallel", "arbitrary"), vmem_limit_bytes=VMEM_LIMIT),
        name="mlstm",
    )(x, g, w_main, wkt, wg, wgt, brow, bcol, hnorm, w_out)


def _rglru_kernel(x_ref, g_ref, win_ref, cw_ref, cb_ref, wai_ref, ba_ref, bi_ref, lam_ref,
                  wout_ref, o_ref, ubuf_ref, a_ref, b_ref, h_ref):
    TM = R_TM

    @pl.when(pl.program_id(1) == 0)
    def _():
        ubuf_ref[0:SUBLANES, :] = jnp.zeros((SUBLANES, R_WIDTH), F32)
        h_ref[...] = jnp.zeros_like(h_ref)

    x = x_ref[0]
    xn = _rms_scale(x, g_ref[...]).astype(BF16)
    u = _dot(xn, win_ref[:, 0:R_WIDTH])
    gate = _dot(xn, win_ref[:, R_WIDTH:2 * R_WIDTH])

    ubuf_ref[SUBLANES:SUBLANES + TM, :] = u
    uc = cb_ref[...] + cw_ref[CONV_W - 1:CONV_W, :] * u
    for j in range(CONV_W - 1):
        off = SUBLANES - (CONV_W - 1) + j
        uc = uc + cw_ref[j:j + 1, :] * ubuf_ref[off:off + TM, :]
    ubuf_ref[0:SUBLANES, :] = ubuf_ref[TM:TM + SUBLANES, :]

    uc_bf = uc.astype(BF16)
    pre_a = []
    pre_i = []
    for sb in range(R_WIDTH // R_SB):
        gsb = _dot(uc_bf[:, sb * R_SB:(sb + 1) * R_SB], wai_ref[sb])
        pre_a.append(gsb[:, 0:R_SB])
        pre_i.append(gsb[:, R_SB:2 * R_SB])
    r = jax.nn.sigmoid(jnp.concatenate(pre_a, axis=1) + ba_ref[...])
    i = jax.nn.sigmoid(jnp.concatenate(pre_i, axis=1) + bi_ref[...])

    log_a = (-RG_C * _softplus(-lam_ref[...])) * r
    a = jnp.exp(log_a)
    mult = jnp.sqrt(-jnp.tanh(log_a) * (a * a + 1.0))
    a_ref[...] = a
    b_ref[...] = mult * (i * uc)

    srow = lax.broadcasted_iota(jnp.int32, (SUBLANES, R_WIDTH), 0)

    def slab(gidx, carry):
        r0 = pl.multiple_of(gidx * SUBLANES, SUBLANES)
        av = a_ref[pl.ds(r0, SUBLANES), :]
        bv = b_ref[pl.ds(r0, SUBLANES), :]
        for d in (1, 2, 4):
            a_sh = jnp.where(srow >= d, pltpu.roll(av, d, 0), 1.0)
            b_sh = jnp.where(srow >= d, pltpu.roll(bv, d, 0), 0.0)
            bv = av * b_sh + bv
            av = av * a_sh
        hv = av * carry + bv
        b_ref[pl.ds(r0, SUBLANES), :] = hv
        return hv[SUBLANES - 1:SUBLANES, :]

    h_last = lax.fori_loop(0, TM // SUBLANES, slab, h_ref[...])
    h_ref[...] = h_last

    y = (b_ref[...] * jax.nn.gelu(gate, approximate=True)).astype(BF16)
    o_ref[0] = x + _dot(y, wout_ref[...])


def _rglru(x, g, w_in, conv_w, conv_b, w_ai, b_a, b_i, lam, w_out):
    B, S, _ = x.shape
    const = lambda b, s: (0, 0)
    return pl.pallas_call(
        _rglru_kernel,
        grid=(B, S // R_TM),
        in_specs=[
            pl.BlockSpec((1, R_TM, D_MODEL), lambda b, s: (b, s, 0)),
            pl.BlockSpec((1, D_MODEL), const),
            pl.BlockSpec((D_MODEL, 2 * R_WIDTH), const),
            pl.BlockSpec((CONV_W, R_WIDTH), const),
            pl.BlockSpec((1, R_WIDTH), const),
            pl.BlockSpec((R_WIDTH // R_SB, R_SB, 2 * R_SB), lambda b, s: (0, 0, 0)),
            pl.BlockSpec((1, R_WIDTH), const),
            pl.BlockSpec((1, R_WIDTH), const),
            pl.BlockSpec((1, R_WIDTH), const),
            pl.BlockSpec((R_WIDTH, D_MODEL), const),
        ],
        out_specs=pl.BlockSpec((1, R_TM, D_MODEL), lambda b, s: (b, s, 0)),
        out_shape=jax.ShapeDtypeStruct((B, S, D_MODEL), F32),
        scratch_shapes=[
            pltpu.VMEM((R_TM + SUBLANES, R_WIDTH), F32),
            pltpu.VMEM((R_TM, R_WIDTH), F32),
            pltpu.VMEM((R_TM, R_WIDTH), F32),
            pltpu.VMEM((1, R_WIDTH), F32),
        ],
        compiler_params=pltpu.CompilerParams(
            dimension_semantics=("parallel", "arbitrary"), vmem_limit_bytes=VMEM_LIMIT),
        name="rglru",
    )(x, g, w_in, conv_w, conv_b, w_ai, b_a, b_i, lam, w_out)


def _superblock_gates(w_a, w_i):
    per = R_SB // R_BW
    out = []
    for sb in range(R_WIDTH // R_SB):
        wa = jax.scipy.linalg.block_diag(*[w_a[sb * per + n] for n in range(per)])
        wi = jax.scipy.linalg.block_diag(*[w_i[sb * per + n] for n in range(per)])
        out.append(jnp.concatenate([wa, wi], axis=1))
    return jnp.stack(out).astype(BF16)


def kernel(x, ff1_norm, ff1_w_in, ff1_w_out, mix_norm, ff2_norm, ff2_w_in, ff2_w_out, m_w_in, m_b_i, m_b_f, m_head_norm, m_w_out, r_w_in, r_conv_w, r_conv_b, r_w_a, r_b_a, r_w_i, r_b_i, r_lam, r_w_out, final_norm):
    B, S, D = x.shape
    depth = ff1_norm.shape[0]
    fg = final_norm.reshape(1, D)
    for layer in range(depth):
        x = _ffn(x.reshape(B * S, D), ff1_norm[layer].reshape(1, D), ff1_w_in[layer].astype(BF16),
                 ff1_w_out[layer].astype(BF16), fg, False).reshape(B, S, D)
        j = layer // 2
        g = mix_norm[layer].reshape(1, D)
        if layer % 2 == 0:
            w = m_w_in[j]
            n_main = 2 * M_QK + 2 * M_V
            w_gate = jnp.pad(w[:, n_main:], ((0, 0), (0, GATE_PAD - 2 * M_HEADS)))
            bias = jnp.concatenate([m_b_i[j], m_b_f[j]])
            x = _mlstm(x, g, w[:, :n_main].astype(BF16), w[:, M_QK:2 * M_QK].T.astype(BF16),
                       w_gate.astype(BF16), w[:, n_main:].T.astype(BF16),
                       jnp.pad(bias, (0, GATE_PAD - 2 * M_HEADS)).reshape(1, GATE_PAD),
                       bias.reshape(2 * M_HEADS, 1), m_head_norm[j].reshape(1, M_V),
                       m_w_out[j].astype(BF16))
        else:
            x = _rglru(x, g, r_w_in[j].astype(BF16), r_conv_w[j].reshape(CONV_W, R_WIDTH),
                       r_conv_b[j].reshape(1, R_WIDTH), _superblock_gates(r_w_a[j], r_w_i[j]),
                       r_b_a[j].reshape(1, R_WIDTH), r_b_i[j].reshape(1, R_WIDTH),
                       r_lam[j].reshape(1, R_WIDTH), r_w_out[j].astype(BF16))
        x = _ffn(x.reshape(B * S, D), ff2_norm[layer].reshape(1, D), ff2_w_in[layer].astype(BF16),
                 ff2_w_out[layer].astype(BF16), fg, layer == depth - 1).reshape(B, S, D)
    return x
```

```python
import functools

import jax
import jax.numpy as jnp
from jax import lax
from jax.experimental import pallas as pl
from jax.experimental.pallas import tpu as pltpu

D_MODEL = 1024
BATCH = 8
EPS = 1e-6
D_FF = 2816
M_HEADS = 8
M_QK = 512
M_V = 1024
M_DK = 64
M_DV = 128
R_WIDTH = 1280
R_BLOCKS = 8
R_BW = 160
CONV_W = 4
RG_C = 8.0

LANES = 128
SUBLANES = 8
VMEM_LIMIT = 56 * 1024 * 1024

FFN_TM = 512
FFN_TF = 256
M_TM = 256
R_TT = 64
R_SEG = 8
R_WIN_N = 256
R_WIN_K = 512
R_WIN_K0 = tuple(min((m * R_WIN_N // R_BW) * R_BW // LANES * LANES, R_WIDTH - R_WIN_K)
                 for m in range(R_WIDTH // R_WIN_N))
for _m, _k0 in enumerate(R_WIN_K0):
    assert _k0 <= (_m * R_WIN_N // R_BW) * R_BW
    assert _k0 + R_WIN_K >= (((_m + 1) * R_WIN_N - 1) // R_BW + 1) * R_BW
M_N_MAIN = 2 * M_QK + 2 * M_V
M_PAIR_N = 4 * M_DK + 4 * M_DV

BF16 = jnp.bfloat16
F32 = jnp.float32
F32_TINY = float(jnp.finfo(jnp.float32).tiny)

assert BATCH == SUBLANES


def _rms_scale(x, g):
    ms = jnp.mean(x * x, axis=-1, keepdims=True)
    return x * lax.rsqrt(ms + EPS) * g


def _dot(a, b):
    return jnp.dot(a, b, preferred_element_type=F32)


def _dot_nt(a, b):
    return lax.dot_general(a, b, (((1,), (1,)), ((), ())), preferred_element_type=F32)


def _softplus(x):
    return jnp.maximum(x, 0.0) + jnp.log1p(jnp.exp(-jnp.abs(x)))


def _log_sigmoid(x):
    return jnp.minimum(x, 0.0) - jnp.log1p(jnp.exp(-jnp.abs(x)))


def _layer_spec(shape, layer, ngrid):
    zeros = (0,) * len(shape)
    if ngrid == 1:
        imap = lambda i: (layer,) + zeros
    else:
        imap = lambda i, j: (layer,) + zeros
    return pl.BlockSpec((None,) + shape, imap, pipeline_mode=pl.Buffered(1))


def _ffn_kernel(x_ref, g_ref, win_ref, wout_ref, fg_ref, o_ref, act_ref, *, final_norm):
    x = x_ref[...]
    xn = _rms_scale(x, g_ref[...]).astype(BF16)
    for c in range(D_FF // FFN_TF):
        lo = c * FFN_TF
        hg = _dot(xn, win_ref[:, lo:lo + FFN_TF])
        hu = _dot(xn, win_ref[:, D_FF + lo:D_FF + lo + FFN_TF])
        act_ref[:, lo:lo + FFN_TF] = (hg * jax.nn.sigmoid(hg) * hu).astype(BF16)
    y = x + 0.5 * _dot(act_ref[...], wout_ref[...])
    if final_norm:
        y = _rms_scale(y, fg_ref[...])
    o_ref[...] = y


def _ffn(x2d, out_cols, in_map, out_map, grid, g, w_in, w_out, layer, final_g, final_norm):
    n = len(grid)
    rows = x2d.shape[0] * x2d.shape[1] // out_cols
    return pl.pallas_call(
        functools.partial(_ffn_kernel, final_norm=final_norm),
        grid=grid,
        in_specs=[
            pl.BlockSpec((FFN_TM, D_MODEL), in_map),
            _layer_spec((1, D_MODEL), layer, n),
            _layer_spec((D_MODEL, 2 * D_FF), layer, n),
            _layer_spec((D_FF, D_MODEL), layer, n),
            pl.BlockSpec((1, D_MODEL), (lambda i: (0, 0)) if n == 1 else (lambda i, j: (0, 0))),
        ],
        out_specs=pl.BlockSpec((FFN_TM, D_MODEL), out_map),
        out_shape=jax.ShapeDtypeStruct((rows, out_cols), F32),
        scratch_shapes=[pltpu.VMEM((FFN_TM, D_FF), BF16)],
        compiler_params=pltpu.CompilerParams(
            dimension_semantics=("parallel",) * n, vmem_limit_bytes=VMEM_LIMIT),
        name="ffn",
    )(x2d, g, w_in, w_out, final_g)


def _mlstm_kernel(x_ref, g_ref, wp_ref, wkt_ref, wgt_ref, bcol_ref, hn_ref, wout_ref,
                  o_ref, c_ref, m_ref, y_ref):
    L = M_TM
    H = M_HEADS
    PW = 2 * M_DK

    @pl.when(pl.program_id(1) == 0)
    def _():
        c_ref[...] = jnp.zeros_like(c_ref)
        m_ref[...] = jnp.zeros_like(m_ref)

    x = x_ref[...]
    xn = _rms_scale(x, g_ref[...]).astype(BF16)

    gr = _dot_nt(wgt_ref[...], xn) + bcol_ref[...]
    ig = gr[0:H, :]
    lf = _log_sigmoid(gr[H:2 * H, :])
    row = lax.broadcasted_iota(jnp.int32, (L, L), 0)
    col = lax.broadcasted_iota(jnp.int32, (L, L), 1)
    causal = col <= row
    upper = (row <= col).astype(BF16)
    lf_hi = lf.astype(BF16)
    lf_r1 = lf - lf_hi.astype(F32)
    lf_mid = lf_r1.astype(BF16)
    lf_lo = (lf_r1 - lf_mid.astype(F32)).astype(BF16)
    b3 = _dot(jnp.concatenate([lf_hi, lf_mid, lf_lo], axis=0), upper)
    b = b3[0:H, :] + b3[H:2 * H, :] + b3[2 * H:3 * H, :]
    r = ig - b
    lane = lax.broadcasted_iota(jnp.int32, (H, L), 1)
    cmax = r
    d = 1
    while d < L:
        cmax = jnp.maximum(cmax, jnp.where(lane >= d, pltpu.roll(cmax, d, 1), -jnp.inf))
        d *= 2
    m_prev = m_ref[:, 0:1]
    m_run = jnp.maximum(cmax, m_prev)
    w_inter = jnp.exp(m_prev - m_run)
    e_negm = jnp.exp(-(b + m_run))
    m_last = m_run[:, L - 1:L]
    decay = w_inter[:, L - 1:L]
    wk = jnp.exp(r - m_last)
    m_ref[...] = jnp.broadcast_to(b[:, L - 1:L] + m_last, (H, LANES))
    cols = jnp.concatenate([m_run, w_inter, e_negm, jnp.zeros((H, L), F32)], axis=0).T

    pair_lane = lax.broadcasted_iota(jnp.int32, (1, PW), 1)
    ones_blk = jnp.ones((L, M_DV), BF16)
    hnorm = hn_ref[...]

    def project(p):
        z = _dot(xn, wp_ref[p])
        q_pair = (z[:, 0:PW] * (M_DK ** -0.5)).astype(BF16)
        k_pair = z[:, PW:2 * PW].astype(BF16)
        v_pair = z[:, 2 * PW:2 * PW + 2 * M_DV].astype(BF16)
        og_pair = z[:, 2 * PW + 2 * M_DV:2 * PW + 4 * M_DV]
        kt_pair = _dot_nt(wkt_ref[p * PW:(p + 1) * PW, :], xn)
        return q_pair, k_pair, v_pair, og_pair, kt_pair

    def heads(p, proj):
        q_pair, k_pair, v_pair, og_pair, kt_pair = proj
        rows = slice(p * PW, (p + 1) * PW)
        c_pair = c_ref[rows, :].astype(BF16)
        hidx = (2 * p, 2 * p + 1)
        q_m, pmat, numden = [], [], []
        for j, h in enumerate(hidx):
            in_head = (pair_lane >= j * M_DK) & (pair_lane < (j + 1) * M_DK)
            q_m.append(jnp.where(in_head, q_pair, jnp.zeros_like(q_pair)))
        for j, h in enumerate(hidx):
            dmat = jnp.exp(jnp.where(causal, r[h:h + 1, :] - cols[:, h:h + 1], -jnp.inf))
            pmat.append((_dot_nt(q_m[j], k_pair) * dmat).astype(BF16))
        for j, h in enumerate(hidx):
            v_aug = jnp.concatenate([v_pair[:, j * M_DV:(j + 1) * M_DV], ones_blk], axis=1)
            qw = (cols[:, H + h:H + h + 1] * q_m[j].astype(F32)).astype(BF16)
            numden.append(_dot(jnp.concatenate([pmat[j], qw], axis=1),
                               jnp.concatenate([v_aug, c_pair], axis=0)))
            kw = (kt_pair[j * M_DK:(j + 1) * M_DK, :] * wk[h:h + 1, :]).astype(BF16)
            hs = slice(M_DK * h, M_DK * (h + 1))
            c_ref[hs, :] = decay[h:h + 1, :] * c_ref[hs, :] + _dot(kw, v_aug)
        for j, h in enumerate(hidx):
            vs = slice(M_DV * h, M_DV * (h + 1))
            den = numden[j][:, M_DV:2 * M_DV]
            hh = numden[j][:, 0:M_DV] * (1.0 / jnp.maximum(jnp.abs(den), cols[:, 2 * H + h:2 * H + h + 1]))
            hh = hh * lax.rsqrt(jnp.mean(hh * hh, axis=1, keepdims=True) + EPS)
            hh = hh * hnorm[:, vs]
            y_ref[:, vs] = (jax.nn.sigmoid(og_pair[:, j * M_DV:(j + 1) * M_DV]) * hh).astype(BF16)

    n_pairs = H // 2
    proj = [project(0), project(1)]
    for p in range(n_pairs):
        heads(p, proj[p])
        if p + 2 < n_pairs:
            proj.append(project(p + 2))

    o_ref[...] = x + _dot(y_ref[...], wout_ref[...])


def _mlstm(x_tm, g, w_pairs, wkt, wgt, bcol, hnorm, w_out, layer, j):
    S = x_tm.shape[0]
    return pl.pallas_call(
        _mlstm_kernel,
        grid=(BATCH, S // M_TM),
        in_specs=[
            pl.BlockSpec((M_TM, D_MODEL), lambda b, s: (s, b)),
            _layer_spec((1, D_MODEL), layer, 2),
            _layer_spec((M_HEADS // 2, D_MODEL, M_PAIR_N), j, 2),
            _layer_spec((M_QK, D_MODEL), j, 2),
            _layer_spec((2 * M_HEADS, D_MODEL), j, 2),
            _layer_spec((2 * M_HEADS, 1), j, 2),
            _layer_spec((1, M_V), j, 2),
            _layer_spec((M_V, D_MODEL), j, 2),
        ],
        out_specs=pl.BlockSpec((M_TM, D_MODEL), lambda b, s: (s, b)),
        out_shape=jax.ShapeDtypeStruct(x_tm.shape, F32),
        scratch_shapes=[
            pltpu.VMEM((M_HEADS * M_DK, 2 * M_DV), F32),
            pltpu.VMEM((M_HEADS, LANES), F32),
            pltpu.VMEM((M_TM, M_V), BF16),
        ],
        compiler_params=pltpu.CompilerParams(
            dimension_semantics=("parallel", "arbitrary"), vmem_limit_bytes=VMEM_LIMIT),
        name="mlstm",
    )(x_tm, g, w_pairs, wkt, wgt, bcol, hnorm, w_out)


def _scan_time(a, bt, h0):
    nseg = R_TT // R_SEG
    rows = lambda t: slice(t * BATCH, (t + 1) * BATCH)
    ca = [[a[rows(k * R_SEG), :]] for k in range(nseg)]
    cb = [[bt[rows(k * R_SEG), :]] for k in range(nseg)]
    for i in range(1, R_SEG):
        for k in range(nseg):
            at = a[rows(k * R_SEG + i), :]
            cb[k].append(at * cb[k][i - 1] + bt[rows(k * R_SEG + i), :])
            ca[k].append(at * ca[k][i - 1])
    h_in = [h0]
    for k in range(nseg):
        h_in.append(ca[k][R_SEG - 1] * h_in[k] + cb[k][R_SEG - 1])
    hs = [ca[k][i] * h_in[k] + cb[k][i] for k in range(nseg) for i in range(R_SEG)]
    return jnp.concatenate(hs, axis=0), h_in[nseg]


def _rglru_kernel(x_ref, g_ref, win_ref, cw_ref, cb_ref, wai_ref, ba_ref, bi_ref, lam_ref,
                  wout_ref, o_ref, ubuf_ref, ucb_ref, h_ref):
    N = R_TT * BATCH
    HIST = (CONV_W - 1) * BATCH
    W = R_WIN_N
    nwin = R_WIDTH // W

    @pl.when(pl.program_id(0) == 0)
    def _():
        ubuf_ref[0:HIST, :] = jnp.zeros((HIST, R_WIDTH), F32)
        h_ref[...] = jnp.zeros_like(h_ref)

    x = x_ref[...].reshape(N, D_MODEL)
    xn = _rms_scale(x, g_ref[...]).astype(BF16)
    neg_c = -RG_C * _softplus(-lam_ref[...])

    def in_proj(c):
        cs = slice(c * W, (c + 1) * W)
        u = _dot(xn, win_ref[:, cs])
        gate = _dot(xn, win_ref[:, R_WIDTH + c * W:R_WIDTH + (c + 1) * W])
        ubuf_ref[HIST:HIST + N, cs] = u
        uc = cb_ref[:, cs] + cw_ref[CONV_W - 1:CONV_W, cs] * u
        for j in range(CONV_W - 1):
            uc = uc + cw_ref[j:j + 1, cs] * ubuf_ref[j * BATCH:j * BATCH + N, cs]
        ubuf_ref[0:HIST, cs] = ubuf_ref[N:N + HIST, cs]
        ucb_ref[:, cs] = uc.astype(BF16)
        return uc, gate

    def gate_proj(m):
        k0 = R_WIN_K0[m]
        gw = _dot(ucb_ref[:, k0:k0 + R_WIN_K], wai_ref[m])
        return gw[:, 0:W], gw[:, W:2 * W]

    def finish(m, uc, gate, pre, acc):
        cs = slice(m * W, (m + 1) * W)
        rg = jax.nn.sigmoid(pre[0] + ba_ref[:, cs])
        ig = jax.nn.sigmoid(pre[1] + bi_ref[:, cs])
        log_a = neg_c[:, cs] * rg
        a = jnp.exp(log_a)
        w = -jnp.tanh(log_a) * (a * a + 1.0)
        bt = (w * lax.rsqrt(jnp.maximum(w, F32_TINY))) * (ig * uc)
        h, h_last = _scan_time(a, bt, h_ref[:, cs])
        h_ref[:, cs] = h_last
        y = (h * jax.nn.gelu(gate, approximate=True)).astype(BF16)
        return acc + _dot(y, wout_ref[cs, :])

    pending = [in_proj(0), in_proj(1)]
    pre = gate_proj(0)
    acc = x
    for m in range(nwin):
        if m + 2 < nwin:
            pending.append(in_proj(m + 2))
        nxt = gate_proj(m + 1) if m + 1 < nwin else None
        acc = finish(m, pending[m][0], pending[m][1], pre, acc)
        pre = nxt
    o_ref[...] = acc.reshape(R_TT, BATCH, D_MODEL)


def _rglru(x_tm, g, w_in, conv_w, conv_b, w_ai, b_a, b_i, lam, w_out, layer, j):
    S = x_tm.shape[0]
    n_rows = R_TT * BATCH
    x3 = x_tm.reshape(S, BATCH, D_MODEL)
    out = pl.pallas_call(
        _rglru_kernel,
        grid=(S // R_TT,),
        in_specs=[
            pl.BlockSpec((R_TT, BATCH, D_MODEL), lambda s: (s, 0, 0)),
            _layer_spec((1, D_MODEL), layer, 1),
            _layer_spec((D_MODEL, 2 * R_WIDTH), j, 1),
            _layer_spec((CONV_W, R_WIDTH), j, 1),
            _layer_spec((1, R_WIDTH), j, 1),
            _layer_spec((len(R_WIN_K0), R_WIN_K, 2 * R_WIN_N), j, 1),
            _layer_spec((1, R_WIDTH), j, 1),
            _layer_spec((1, R_WIDTH), j, 1),
            _layer_spec((1, R_WIDTH), j, 1),
            _layer_spec((R_WIDTH, D_MODEL), j, 1),
        ],
        out_specs=pl.BlockSpec((R_TT, BATCH, D_MODEL), lambda s: (s, 0, 0)),
        out_shape=jax.ShapeDtypeStruct(x3.shape, F32),
        scratch_shapes=[
            pltpu.VMEM((n_rows + (CONV_W - 1) * BATCH, R_WIDTH), F32),
            pltpu.VMEM((n_rows, R_WIDTH), BF16),
            pltpu.VMEM((BATCH, R_WIDTH), F32),
        ],
        compiler_params=pltpu.CompilerParams(
            dimension_semantics=("arbitrary",), vmem_limit_bytes=VMEM_LIMIT),
        name="rglru",
    )(x3, g, w_in, conv_w, conv_b, w_ai, b_a, b_i, lam, w_out)
    return out.reshape(x_tm.shape)


def _window_gates(w_a, w_i):
    nl = w_a.shape[0]
    eye = jnp.eye(R_BLOCKS, dtype=w_a.dtype)

    def dense(w):
        full = w[:, :, :, None, :] * eye[None, :, None, :, None]
        return full.reshape(nl, R_WIDTH, R_WIDTH)

    da, di = dense(w_a), dense(w_i)
    wins = []
    for m, k0 in enumerate(R_WIN_K0):
        cols = slice(m * R_WIN_N, (m + 1) * R_WIN_N)
        wins.append(jnp.concatenate([da[:, k0:k0 + R_WIN_K, cols], di[:, k0:k0 + R_WIN_K, cols]], axis=-1))
    return jnp.stack(wins, axis=1).astype(BF16)


def kernel(x, ff1_norm, ff1_w_in, ff1_w_out, mix_norm, ff2_norm, ff2_w_in, ff2_w_out, m_w_in, m_b_i, m_b_f, m_head_norm, m_w_out, r_w_in, r_conv_w, r_conv_b, r_w_a, r_b_a, r_w_i, r_b_i, r_lam, r_w_out, final_norm):
    B, S, D = x.shape
    depth = ff1_norm.shape[0]
    nt = S // FFN_TM
    fg = final_norm.reshape(1, D)
    ff1_g, mix_g, ff2_g = (a.reshape(depth, 1, D) for a in (ff1_norm, mix_norm, ff2_norm))
    ff1_wi, ff1_wo, ff2_wi, ff2_wo = (a.astype(BF16) for a in (ff1_w_in, ff1_w_out, ff2_w_in, ff2_w_out))
    m_pairs = jnp.stack([jnp.concatenate([
        m_w_in[:, :, 2 * M_DK * p:2 * M_DK * (p + 1)],
        m_w_in[:, :, M_QK + 2 * M_DK * p:M_QK + 2 * M_DK * (p + 1)],
        m_w_in[:, :, 2 * M_QK + 2 * M_DV * p:2 * M_QK + 2 * M_DV * (p + 1)],
        m_w_in[:, :, 2 * M_QK + M_V + 2 * M_DV * p:2 * M_QK + M_V + 2 * M_DV * (p + 1)],
    ], axis=-1) for p in range(M_HEADS // 2)], axis=1).astype(BF16)
    m_wkt = jnp.swapaxes(m_w_in[:, :, M_QK:2 * M_QK], 1, 2).astype(BF16)
    m_wgt = jnp.swapaxes(m_w_in[:, :, M_N_MAIN:], 1, 2).astype(BF16)
    m_bcol = jnp.concatenate([m_b_i, m_b_f], axis=1)[:, :, None]
    m_hn = m_head_norm[:, None, :]
    m_wo = m_w_out.astype(BF16)
    r_wi = r_w_in.astype(BF16)
    r_cw = r_conv_w.reshape(-1, CONV_W, R_WIDTH)
    r_wai = _window_gates(r_w_a, r_w_i)
    r_cb, r_ba, r_bi, r_l = (a[:, None, :] for a in (r_conv_b, r_b_a, r_b_i, r_lam))
    r_wo = r_w_out.astype(BF16)

    tm_map = lambda i: (i, 0)
    for layer in range(depth):
        if layer == 0:
            xt = _ffn(x.reshape(B * S, D), B * D, lambda b, i: (b * nt + i, 0), lambda b, i: (i, b),
                      (B, nt), ff1_g, ff1_wi, ff1_wo, layer, fg, False)
        else:
            xt = _ffn(xt.reshape(S * B, D), D, tm_map, tm_map, (S * B // FFN_TM,),
                      ff1_g, ff1_wi, ff1_wo, layer, fg, False).reshape(S, B * D)
        j = layer // 2
        if layer % 2 == 0:
            xt = _mlstm(xt, mix_g, m_pairs, m_wkt, m_wgt, m_bcol, m_hn, m_wo, layer, j)
        else:
            xt = _rglru(xt, mix_g, r_wi, r_cw, r_cb, r_wai, r_ba, r_bi, r_l, r_wo, layer, j)
        if layer == depth - 1:
            out = _ffn(xt, D, lambda b, i: (i, b), lambda b, i: (b * nt + i, 0),
                       (B, nt), ff2_g, ff2_wi, ff2_wo, layer, fg, True)
        else:
            xt = _ffn(xt.reshape(S * B, D), D, tm_map, tm_map, (S * B // FFN_TM,),
                      ff2_g, ff2_wi, ff2_wo, layer, fg, False).reshape(S, B * D)
    return out.reshape(B, S, D)
```

```python
import functools

import jax
import jax.numpy as jnp
from jax import lax
from jax.experimental import pallas as pl
from jax.experimental.pallas import tpu as pltpu

D_MODEL = 1024
BATCH = 8
EPS = 1e-6
D_FF = 2816
M_HEADS = 8
M_QK = 512
M_V = 1024
M_DK = 64
M_DV = 128
R_WIDTH = 1280
R_BLOCKS = 8
R_BW = 160
CONV_W = 4
RG_C = 8.0

LANES = 128
SUBLANES = 8
VMEM_LIMIT = 56 * 1024 * 1024

FFN_TM = 512
FFN_TT = FFN_TM // BATCH
FFN_TF = 256
FFN_TR = 128
FFN_TN = 256
M_TM = 256
R_TT = 64
R_SEG = 8
R_WIN_N = 256
R_WIN_K = 512
R_WIN_K0 = tuple(min((m * R_WIN_N // R_BW) * R_BW // LANES * LANES, R_WIDTH - R_WIN_K)
                 for m in range(R_WIDTH // R_WIN_N))
for _m, _k0 in enumerate(R_WIN_K0):
    assert _k0 <= (_m * R_WIN_N // R_BW) * R_BW
    assert _k0 + R_WIN_K >= (((_m + 1) * R_WIN_N - 1) // R_BW + 1) * R_BW
M_N_MAIN = 2 * M_QK + 2 * M_V
M_PAIR_N = 4 * M_DK + 4 * M_DV

BF16 = jnp.bfloat16
F32 = jnp.float32
F32_TINY = float(jnp.finfo(jnp.float32).tiny)

assert BATCH == SUBLANES


def _rms_scale(x, g):
    ms = jnp.mean(x * x, axis=-1, keepdims=True)
    return x * lax.rsqrt(ms + EPS) * g


def _dot(a, b):
    return jnp.dot(a, b, preferred_element_type=F32)


def _dot_nt(a, b):
    return lax.dot_general(a, b, (((1,), (1,)), ((), ())), preferred_element_type=F32)


def _softplus(x):
    return jnp.maximum(x, 0.0) + jnp.log1p(jnp.exp(-jnp.abs(x)))


def _log_sigmoid(x):
    return jnp.minimum(x, 0.0) - jnp.log1p(jnp.exp(-jnp.abs(x)))


def _layer_spec(shape, layer, ngrid):
    zeros = (0,) * len(shape)
    if ngrid == 1:
        imap = lambda i: (layer,) + zeros
    else:
        imap = lambda i, j: (layer,) + zeros
    return pl.BlockSpec((None,) + shape, imap, pipeline_mode=pl.Buffered(1))


def _ffn_kernel(x_ref, g_ref, win_ref, wout_ref, fg_ref, o_ref, act_ref, *, final_norm, swap):
    lead, per, _ = x_ref.shape
    nb = FFN_TR // per
    g = g_ref[...]

    def put(cols, y):
        y3 = y.reshape(lead, per, y.shape[-1])
        o_ref[:, :, cols] = jnp.swapaxes(y3, 0, 1) if swap else y3

    blocks = []
    for r in range(0, FFN_TM, FFN_TR):
        x_r = x_ref[r // per:r // per + nb].reshape(FFN_TR, D_MODEL)
        xn_r = _rms_scale(x_r, g).astype(BF16)
        hg = _dot(xn_r, win_ref[:, 0:FFN_TF])
        hu = _dot(xn_r, win_ref[:, D_FF:D_FF + FFN_TF])
        act_ref[r:r + FFN_TR, 0:FFN_TF] = (hg * jax.nn.sigmoid(hg) * hu).astype(BF16)
        blocks.append(xn_r)
    xn = jnp.concatenate(blocks, axis=0)
    for c in range(1, D_FF // FFN_TF):
        lo = c * FFN_TF
        hg = _dot(xn, win_ref[:, lo:lo + FFN_TF])
        hu = _dot(xn, win_ref[:, D_FF + lo:D_FF + lo + FFN_TF])
        act_ref[:, lo:lo + FFN_TF] = (hg * jax.nn.sigmoid(hg) * hu).astype(BF16)
    if final_norm:
        y = x_ref[...].reshape(FFN_TM, D_MODEL) + 0.5 * _dot(act_ref[...], wout_ref[...])
        put(slice(0, D_MODEL), _rms_scale(y, fg_ref[...]))
    else:
        for n in range(0, D_MODEL, FFN_TN):
            cols = slice(n, n + FFN_TN)
            put(cols, x_ref[:, :, cols].reshape(FFN_TM, FFN_TN) + 0.5 * _dot(act_ref[...], wout_ref[:, cols]))


def _ffn(x3, in_tm, out_tm, g, w_in, w_out, layer, final_g, final_norm):
    tm_shape, bm_shape = (FFN_TT, BATCH, D_MODEL), (BATCH, FFN_TT, D_MODEL)
    tm_map, bm_map = (lambda i: (i, 0, 0)), (lambda i: (0, i, 0))
    S = x3.shape[0] if in_tm else x3.shape[1]
    out_shape = (S, BATCH, D_MODEL) if out_tm else (BATCH, S, D_MODEL)
    return pl.pallas_call(
        functools.partial(_ffn_kernel, final_norm=final_norm, swap=in_tm != out_tm),
        grid=(S // FFN_TT,),
        in_specs=[
            pl.BlockSpec(tm_shape if in_tm else bm_shape, tm_map if in_tm else bm_map),
            _layer_spec((1, D_MODEL), layer, 1),
            _layer_spec((D_MODEL, 2 * D_FF), layer, 1),
            _layer_spec((D_FF, D_MODEL), layer, 1),
            pl.BlockSpec((1, D_MODEL), lambda i: (0, 0)),
        ],
        out_specs=pl.BlockSpec(tm_shape if out_tm else bm_shape, tm_map if out_tm else bm_map),
        out_shape=jax.ShapeDtypeStruct(out_shape, F32),
        scratch_shapes=[pltpu.VMEM((FFN_TM, D_FF), BF16)],
        compiler_params=pltpu.CompilerParams(
            dimension_semantics=("parallel",), vmem_limit_bytes=VMEM_LIMIT),
        name="ffn",
    )(x3, g, w_in, w_out, final_g)


def _mlstm_kernel(x_ref, g_ref, wp_ref, wkt_ref, wgt_ref, bcol_ref, hn_ref, wout_ref,
                  o_ref, c_ref, m_ref, y_ref):
    L = M_TM
    H = M_HEADS
    PW = 2 * M_DK

    @pl.when(pl.program_id(1) == 0)
    def _():
        c_ref[...] = jnp.zeros_like(c_ref)
        m_ref[...] = jnp.zeros_like(m_ref)

    x = x_ref[...]
    xn = _rms_scale(x, g_ref[...]).astype(BF16)

    gr = _dot_nt(wgt_ref[...], xn) + bcol_ref[...]
    ig = gr[0:H, :]
    lf = _log_sigmoid(gr[H:2 * H, :])
    row = lax.broadcasted_iota(jnp.int32, (L, L), 0)
    col = lax.broadcasted_iota(jnp.int32, (L, L), 1)
    causal = col <= row
    upper = (row <= col).astype(BF16)
    lf_hi = lf.astype(BF16)
    lf_r1 = lf - lf_hi.astype(F32)
    lf_mid = lf_r1.astype(BF16)
    lf_lo = (lf_r1 - lf_mid.astype(F32)).astype(BF16)
    b3 = _dot(jnp.concatenate([lf_hi, lf_mid, lf_lo], axis=0), upper)
    b = b3[0:H, :] + b3[H:2 * H, :] + b3[2 * H:3 * H, :]
    r = ig - b
    lane = lax.broadcasted_iota(jnp.int32, (H, L), 1)
    cmax = r
    d = 1
    while d < L:
        cmax = jnp.maximum(cmax, jnp.where(lane >= d, pltpu.roll(cmax, d, 1), -jnp.inf))
        d *= 2
    m_prev = m_ref[:, 0:1]
    m_run = jnp.maximum(cmax, m_prev)
    w_inter = jnp.exp(m_prev - m_run)
    e_negm = jnp.exp(-(b + m_run))
    m_last = m_run[:, L - 1:L]
    decay = w_inter[:, L - 1:L]
    wk = jnp.exp(r - m_last)
    m_ref[...] = jnp.broadcast_to(b[:, L - 1:L] + m_last, (H, LANES))
    cols = jnp.concatenate([m_run, w_inter, e_negm, jnp.zeros((H, L), F32)], axis=0).T

    pair_lane = lax.broadcasted_iota(jnp.int32, (1, PW), 1)
    ones_blk = jnp.ones((L, M_DV), BF16)
    hnorm = hn_ref[...]

    def project(p):
        z = _dot(xn, wp_ref[p])
        q_pair = (z[:, 0:PW] * (M_DK ** -0.5)).astype(BF16)
        k_pair = z[:, PW:2 * PW].astype(BF16)
        v_pair = z[:, 2 * PW:2 * PW + 2 * M_DV].astype(BF16)
        og_pair = z[:, 2 * PW + 2 * M_DV:2 * PW + 4 * M_DV]
        kt_pair = _dot_nt(wkt_ref[p * PW:(p + 1) * PW, :], xn)
        return q_pair, k_pair, v_pair, og_pair, kt_pair

    def heads(p, proj):
        q_pair, k_pair, v_pair, og_pair, kt_pair = proj
        rows = slice(p * PW, (p + 1) * PW)
        c_pair = c_ref[rows, :].astype(BF16)
        hidx = (2 * p, 2 * p + 1)
        q_m, pmat, numden = [], [], []
        for j, h in enumerate(hidx):
            in_head = (pair_lane >= j * M_DK) & (pair_lane < (j + 1) * M_DK)
            q_m.append(jnp.where(in_head, q_pair, jnp.zeros_like(q_pair)))
        for j, h in enumerate(hidx):
            dmat = jnp.exp(jnp.where(causal, r[h:h + 1, :] - cols[:, h:h + 1], -jnp.inf))
            pmat.append((_dot_nt(q_m[j], k_pair) * dmat).astype(BF16))
        for j, h in enumerate(hidx):
            v_aug = jnp.concatenate([v_pair[:, j * M_DV:(j + 1) * M_DV], ones_blk], axis=1)
            qw = (cols[:, H + h:H + h + 1] * q_m[j].astype(F32)).astype(BF16)
            numden.append(_dot(jnp.concatenate([pmat[j], qw], axis=1),
                               jnp.concatenate([v_aug, c_pair], axis=0)))
            kw = (kt_pair[j * M_DK:(j + 1) * M_DK, :] * wk[h:h + 1, :]).astype(BF16)
            hs = slice(M_DK * h, M_DK * (h + 1))
            c_ref[hs, :] = decay[h:h + 1, :] * c_ref[hs, :] + _dot(kw, v_aug)
        for j, h in enumerate(hidx):
            vs = slice(M_DV * h, M_DV * (h + 1))
            den = numden[j][:, M_DV:2 * M_DV]
            hh = numden[j][:, 0:M_DV] * (1.0 / jnp.maximum(jnp.abs(den), cols[:, 2 * H + h:2 * H + h + 1]))
            hh = hh * lax.rsqrt(jnp.mean(hh * hh, axis=1, keepdims=True) + EPS)
            hh = hh * hnorm[:, vs]
            y_ref[:, vs] = (jax.nn.sigmoid(og_pair[:, j * M_DV:(j + 1) * M_DV]) * hh).astype(BF16)

    n_pairs = H // 2
    proj = [project(0), project(1)]
    for p in range(n_pairs):
        heads(p, proj[p])
        if p + 2 < n_pairs:
            proj.append(project(p + 2))

    o_ref[...] = x + _dot(y_ref[...], wout_ref[...])


def _mlstm(x2d, g, w_pairs, wkt, wgt, bcol, hnorm, w_out, layer, j):
    nt = x2d.shape[0] // BATCH // M_TM
    return pl.pallas_call(
        _mlstm_kernel,
        grid=(BATCH, nt),
        in_specs=[
            pl.BlockSpec((M_TM, D_MODEL), lambda b, s: (b * nt + s, 0)),
            _layer_spec((1, D_MODEL), layer, 2),
            _layer_spec((M_HEADS // 2, D_MODEL, M_PAIR_N), j, 2),
            _layer_spec((M_QK, D_MODEL), j, 2),
            _layer_spec((2 * M_HEADS, D_MODEL), j, 2),
            _layer_spec((2 * M_HEADS, 1), j, 2),
            _layer_spec((1, M_V), j, 2),
            _layer_spec((M_V, D_MODEL), j, 2),
        ],
        out_specs=pl.BlockSpec((M_TM, D_MODEL), lambda b, s: (b * nt + s, 0)),
        out_shape=jax.ShapeDtypeStruct(x2d.shape, F32),
        scratch_shapes=[
            pltpu.VMEM((M_HEADS * M_DK, 2 * M_DV), F32),
            pltpu.VMEM((M_HEADS, LANES), F32),
            pltpu.VMEM((M_TM, M_V), BF16),
        ],
        compiler_params=pltpu.CompilerParams(
            dimension_semantics=("parallel", "arbitrary"), vmem_limit_bytes=VMEM_LIMIT),
        name="mlstm",
    )(x2d, g, w_pairs, wkt, wgt, bcol, hnorm, w_out)


def _scan_time(a, bt, h0):
    nseg = R_TT // R_SEG
    rows = lambda t: slice(t * BATCH, (t + 1) * BATCH)
    ca = [[a[rows(k * R_SEG), :]] for k in range(nseg)]
    cb = [[bt[rows(k * R_SEG), :]] for k in range(nseg)]
    for i in range(1, R_SEG):
        for k in range(nseg):
            at = a[rows(k * R_SEG + i), :]
            cb[k].append(at * cb[k][i - 1] + bt[rows(k * R_SEG + i), :])
            ca[k].append(at * ca[k][i - 1])
    h_in = [h0]
    for k in range(nseg):
        h_in.append(ca[k][R_SEG - 1] * h_in[k] + cb[k][R_SEG - 1])
    hs = [ca[k][i] * h_in[k] + cb[k][i] for k in range(nseg) for i in range(R_SEG)]
    return jnp.concatenate(hs, axis=0), h_in[nseg]


def _rglru_kernel(x_ref, g_ref, win_ref, cw_ref, cb_ref, wai_ref, ba_ref, bi_ref, lam_ref,
                  wout_ref, o_ref, ubuf_ref, ucb_ref, h_ref):
    N = R_TT * BATCH
    HIST = (CONV_W - 1) * BATCH
    W = R_WIN_N
    nwin = R_WIDTH // W

    @pl.when(pl.program_id(0) == 0)
    def _():
        ubuf_ref[0:HIST, :] = jnp.zeros((HIST, R_WIDTH), F32)
        h_ref[...] = jnp.zeros_like(h_ref)

    x = x_ref[...].reshape(N, D_MODEL)
    xn = _rms_scale(x, g_ref[...]).astype(BF16)
    neg_c = -RG_C * _softplus(-lam_ref[...])

    def in_proj(c):
        cs = slice(c * W, (c + 1) * W)
        u = _dot(xn, win_ref[:, cs])
        gate = _dot(xn, win_ref[:, R_WIDTH + c * W:R_WIDTH + (c + 1) * W])
        ubuf_ref[HIST:HIST + N, cs] = u
        uc = cb_ref[:, cs] + cw_ref[CONV_W - 1:CONV_W, cs] * u
        for j in range(CONV_W - 1):
            uc = uc + cw_ref[j:j + 1, cs] * ubuf_ref[j * BATCH:j * BATCH + N, cs]
        ubuf_ref[0:HIST, cs] = ubuf_ref[N:N + HIST, cs]
        ucb_ref[:, cs] = uc.astype(BF16)
        return uc, gate

    def gate_proj(m):
        k0 = R_WIN_K0[m]
        gw = _dot(ucb_ref[:, k0:k0 + R_WIN_K], wai_ref[m])
        return gw[:, 0:W], gw[:, W:2 * W]

    def finish(m, uc, gate, pre, acc):
        cs = slice(m * W, (m + 1) * W)
        rg = jax.nn.sigmoid(pre[0] + ba_ref[:, cs])
        ig = jax.nn.sigmoid(pre[1] + bi_ref[:, cs])
        log_a = neg_c[:, cs] * rg
        a = jnp.exp(log_a)
        w = -jnp.tanh(log_a) * (a * a + 1.0)
        bt = (w * lax.rsqrt(jnp.maximum(w, F32_TINY))) * (ig * uc)
        h, h_last = _scan_time(a, bt, h_ref[:, cs])
        h_ref[:, cs] = h_last
        y = (h * jax.nn.gelu(gate, approximate=True)).astype(BF16)
        return acc + _dot(y, wout_ref[cs, :])

    pending = [in_proj(0), in_proj(1)]
    pre = gate_proj(0)
    acc = x
    for m in range(nwin):
        if m + 2 < nwin:
            pending.append(in_proj(m + 2))
        nxt = gate_proj(m + 1) if m + 1 < nwin else None
        acc = finish(m, pending[m][0], pending[m][1], pre, acc)
        pre = nxt
    o_ref[...] = acc.reshape(R_TT, BATCH, D_MODEL)


def _rglru(x3, g, w_in, conv_w, conv_b, w_ai, b_a, b_i, lam, w_out, layer, j):
    S = x3.shape[0]
    n_rows = R_TT * BATCH
    return pl.pallas_call(
        _rglru_kernel,
        grid=(S // R_TT,),
        in_specs=[
            pl.BlockSpec((R_TT, BATCH, D_MODEL), lambda s: (s, 0, 0)),
            _layer_spec((1, D_MODEL), layer, 1),
            _layer_spec((D_MODEL, 2 * R_WIDTH), j, 1),
            _layer_spec((CONV_W, R_WIDTH), j, 1),
            _layer_spec((1, R_WIDTH), j, 1),
            _layer_spec((len(R_WIN_K0), R_WIN_K, 2 * R_WIN_N), j, 1),
            _layer_spec((1, R_WIDTH), j, 1),
            _layer_spec((1, R_WIDTH), j, 1),
            _layer_spec((1, R_WIDTH), j, 1),
            _layer_spec((R_WIDTH, D_MODEL), j, 1),
        ],
        out_specs=pl.BlockSpec((R_TT, BATCH, D_MODEL), lambda s: (s, 0, 0)),
        out_shape=jax.ShapeDtypeStruct(x3.shape, F32),
        scratch_shapes=[
            pltpu.VMEM((n_rows + (CONV_W - 1) * BATCH, R_WIDTH), F32),
            pltpu.VMEM((n_rows, R_WIDTH), BF16),
            pltpu.VMEM((BATCH, R_WIDTH), F32),
        ],
        compiler_params=pltpu.CompilerParams(
            dimension_semantics=("arbitrary",), vmem_limit_bytes=VMEM_LIMIT),
        name="rglru",
    )(x3, g, w_in, conv_w, conv_b, w_ai, b_a, b_i, lam, w_out)


def _window_gates(w_a, w_i):
    nl = w_a.shape[0]
    eye = jnp.eye(R_BLOCKS, dtype=w_a.dtype)

    def dense(w):
        full = w[:, :, :, None, :] * eye[None, :, None, :, None]
        return full.reshape(nl, R_WIDTH, R_WIDTH)

    da, di = dense(w_a), dense(w_i)
    wins = []
    for m, k0 in enumerate(R_WIN_K0):
        cols = slice(m * R_WIN_N, (m + 1) * R_WIN_N)
        wins.append(jnp.concatenate([da[:, k0:k0 + R_WIN_K, cols], di[:, k0:k0 + R_WIN_K, cols]], axis=-1))
    return jnp.stack(wins, axis=1).astype(BF16)


def kernel(x, ff1_norm, ff1_w_in, ff1_w_out, mix_norm, ff2_norm, ff2_w_in, ff2_w_out, m_w_in, m_b_i, m_b_f, m_head_norm, m_w_out, r_w_in, r_conv_w, r_conv_b, r_w_a, r_b_a, r_w_i, r_b_i, r_lam, r_w_out, final_norm):
    B, S, D = x.shape
    depth = ff1_norm.shape[0]
    fg = final_norm.reshape(1, D)
    ff1_g, mix_g, ff2_g = (a.reshape(depth, 1, D) for a in (ff1_norm, mix_norm, ff2_norm))
    ff1_wi, ff1_wo, ff2_wi, ff2_wo = (a.astype(BF16) for a in (ff1_w_in, ff1_w_out, ff2_w_in, ff2_w_out))
    m_pairs = jnp.stack([jnp.concatenate([
        m_w_in[:, :, 2 * M_DK * p:2 * M_DK * (p + 1)],
        m_w_in[:, :, M_QK + 2 * M_DK * p:M_QK + 2 * M_DK * (p + 1)],
        m_w_in[:, :, 2 * M_QK + 2 * M_DV * p:2 * M_QK + 2 * M_DV * (p + 1)],
        m_w_in[:, :, 2 * M_QK + M_V + 2 * M_DV * p:2 * M_QK + M_V + 2 * M_DV * (p + 1)],
    ], axis=-1) for p in range(M_HEADS // 2)], axis=1).astype(BF16)
    m_wkt = jnp.swapaxes(m_w_in[:, :, M_QK:2 * M_QK], 1, 2).astype(BF16)
    m_wgt = jnp.swapaxes(m_w_in[:, :, M_N_MAIN:], 1, 2).astype(BF16)
    m_bcol = jnp.concatenate([m_b_i, m_b_f], axis=1)[:, :, None]
    m_hn = m_head_norm[:, None, :]
    m_wo = m_w_out.astype(BF16)
    r_wi = r_w_in.astype(BF16)
    r_cw = r_conv_w.reshape(-1, CONV_W, R_WIDTH)
    r_wai = _window_gates(r_w_a, r_w_i)
    r_cb, r_ba, r_bi, r_l = (a[:, None, :] for a in (r_conv_b, r_b_a, r_b_i, r_lam))
    r_wo = r_w_out.astype(BF16)

    xb = x
    for layer in range(depth):
        j = layer // 2
        if layer % 2 == 0:
            xb = _ffn(xb, False, False, ff1_g, ff1_wi, ff1_wo, layer, fg, False)
            xb = _mlstm(xb.reshape(B * S, D), mix_g, m_pairs, m_wkt, m_wgt, m_bcol, m_hn, m_wo,
                        layer, j).reshape(B, S, D)
            xb = _ffn(xb, False, False, ff2_g, ff2_wi, ff2_wo, layer, fg, layer == depth - 1)
        else:
            xt = _ffn(xb, False, True, ff1_g, ff1_wi, ff1_wo, layer, fg, False)
            xt = _rglru(xt, mix_g, r_wi, r_cw, r_cb, r_wai, r_ba, r_bi, r_l, r_wo, layer, j)
            xb = _ffn(xt, True, False, ff2_g, ff2_wi, ff2_wo, layer, fg, layer == depth - 1)
    return xb
```

```python
import functools

import jax
import jax.numpy as jnp
from jax import lax
from jax.experimental import pallas as pl
from jax.experimental.pallas import tpu as pltpu

D_MODEL = 1024
BATCH = 8
EPS = 1e-6
D_FF = 2816
M_HEADS = 8
M_QK = 512
M_V = 1024
M_DK = 64
M_DV = 128
R_WIDTH = 1280
R_BLOCKS = 8
R_BW = 160
CONV_W = 4
RG_C = 8.0

LANES = 128
SUBLANES = 8
VMEM_LIMIT = 56 * 1024 * 1024

FFN_TM = 1024
FFN_TT = FFN_TM // BATCH
FFN_TF = 256
FFN_TR = 128
FFN_TN = 256
M_L = 256
M_TM = 512
R_TT = 128
R_SEG = 8
R_WIN_N = 256
R_WIN_K = 512
R_WIN_K0 = tuple(min((m * R_WIN_N // R_BW) * R_BW // LANES * LANES, R_WIDTH - R_WIN_K)
                 for m in range(R_WIDTH // R_WIN_N))
for _m, _k0 in enumerate(R_WIN_K0):
    assert _k0 <= (_m * R_WIN_N // R_BW) * R_BW
    assert _k0 + R_WIN_K >= (((_m + 1) * R_WIN_N - 1) // R_BW + 1) * R_BW
M_N_MAIN = 2 * M_QK + 2 * M_V
M_PAIR_N = 4 * M_DK + 4 * M_DV

BF16 = jnp.bfloat16
F32 = jnp.float32
F32_TINY = float(jnp.finfo(jnp.float32).tiny)

assert BATCH == SUBLANES


def _rms_scale(x, g):
    ms = jnp.mean(x * x, axis=-1, keepdims=True)
    return x * lax.rsqrt(ms + EPS) * g


def _dot(a, b):
    return jnp.dot(a, b, preferred_element_type=F32)


def _dot_nt(a, b):
    return lax.dot_general(a, b, (((1,), (1,)), ((), ())), preferred_element_type=F32)


def _softplus(x):
    return jnp.maximum(x, 0.0) + jnp.log1p(jnp.exp(-jnp.abs(x)))


def _log_sigmoid(x):
    return jnp.minimum(x, 0.0) - jnp.log1p(jnp.exp(-jnp.abs(x)))


def _layer_spec(shape, layer, ngrid):
    zeros = (0,) * len(shape)
    if ngrid == 1:
        imap = lambda i: (layer,) + zeros
    else:
        imap = lambda i, j: (layer,) + zeros
    return pl.BlockSpec((None,) + shape, imap, pipeline_mode=pl.Buffered(1))


def _ffn_kernel(x_ref, g_ref, win_ref, wout_ref, fg_ref, o_ref, act_ref, *, final_norm, swap):
    lead, per, _ = x_ref.shape
    nb = FFN_TR // per
    g = g_ref[...]

    def put(cols, y):
        y3 = y.reshape(lead, per, y.shape[-1])
        o_ref[:, :, cols] = jnp.swapaxes(y3, 0, 1) if swap else y3

    blocks = []
    for r in range(0, FFN_TM, FFN_TR):
        x_r = x_ref[r // per:r // per + nb].reshape(FFN_TR, D_MODEL)
        xn_r = _rms_scale(x_r, g).astype(BF16)
        hg = _dot(xn_r, win_ref[:, 0:FFN_TF])
        hu = _dot(xn_r, win_ref[:, D_FF:D_FF + FFN_TF])
        act_ref[r:r + FFN_TR, 0:FFN_TF] = (hg * jax.nn.sigmoid(hg) * hu).astype(BF16)
        blocks.append(xn_r)
    xn = jnp.concatenate(blocks, axis=0)
    for c in range(1, D_FF // FFN_TF):
        lo = c * FFN_TF
        hg = _dot(xn, win_ref[:, lo:lo + FFN_TF])
        hu = _dot(xn, win_ref[:, D_FF + lo:D_FF + lo + FFN_TF])
        act_ref[:, lo:lo + FFN_TF] = (hg * jax.nn.sigmoid(hg) * hu).astype(BF16)
    if final_norm:
        y = x_ref[...].reshape(FFN_TM, D_MODEL) + 0.5 * _dot(act_ref[...], wout_ref[...])
        put(slice(0, D_MODEL), _rms_scale(y, fg_ref[...]))
    else:
        for n in range(0, D_MODEL, FFN_TN):
            cols = slice(n, n + FFN_TN)
            put(cols, x_ref[:, :, cols].reshape(FFN_TM, FFN_TN) + 0.5 * _dot(act_ref[...], wout_ref[:, cols]))


def _ffn(x3, in_tm, out_tm, g, w_in, w_out, layer, final_g, final_norm):
    tm_shape, bm_shape = (FFN_TT, BATCH, D_MODEL), (BATCH, FFN_TT, D_MODEL)
    tm_map, bm_map = (lambda i: (i, 0, 0)), (lambda i: (0, i, 0))
    S = x3.shape[0] if in_tm else x3.shape[1]
    out_shape = (S, BATCH, D_MODEL) if out_tm else (BATCH, S, D_MODEL)
    return pl.pallas_call(
        functools.partial(_ffn_kernel, final_norm=final_norm, swap=in_tm != out_tm),
        grid=(S // FFN_TT,),
        in_specs=[
            pl.BlockSpec(tm_shape if in_tm else bm_shape, tm_map if in_tm else bm_map),
            _layer_spec((1, D_MODEL), layer, 1),
            _layer_spec((D_MODEL, 2 * D_FF), layer, 1),
            _layer_spec((D_FF, D_MODEL), layer, 1),
            pl.BlockSpec((1, D_MODEL), lambda i: (0, 0)),
        ],
        out_specs=pl.BlockSpec(tm_shape if out_tm else bm_shape, tm_map if out_tm else bm_map),
        out_shape=jax.ShapeDtypeStruct(out_shape, F32),
        scratch_shapes=[pltpu.VMEM((FFN_TM, D_FF), BF16)],
        compiler_params=pltpu.CompilerParams(
            dimension_semantics=("parallel",), vmem_limit_bytes=VMEM_LIMIT),
        name="ffn",
    )(x3, g, w_in, w_out, final_g)


def _mlstm_kernel(x_ref, g_ref, wp_ref, wgt_ref, bcol_ref, hn_ref, wout_ref,
                  o_ref, c_ref, m_ref, y_ref):
    L = M_L
    H = M_HEADS
    PW = 2 * M_DK
    n_chunks = M_TM // L

    @pl.when(pl.program_id(1) == 0)
    def _():
        c_ref[...] = jnp.zeros_like(c_ref)
        m_ref[...] = jnp.zeros_like(m_ref)

    x = x_ref[...]
    xn = _rms_scale(x, g_ref[...]).astype(BF16)

    gr = _dot_nt(wgt_ref[...], xn) + bcol_ref[...]
    row = lax.broadcasted_iota(jnp.int32, (L, L), 0)
    col = lax.broadcasted_iota(jnp.int32, (L, L), 1)
    causal = col <= row
    upper = (row <= col).astype(BF16)
    lane = lax.broadcasted_iota(jnp.int32, (H, L), 1)

    def chunk_rows(c, m_prev):
        ig = gr[0:H, c * L:(c + 1) * L]
        lf = _log_sigmoid(gr[H:2 * H, c * L:(c + 1) * L])
        lf_hi = lf.astype(BF16)
        lf_r1 = lf - lf_hi.astype(F32)
        lf_mid = lf_r1.astype(BF16)
        lf_lo = (lf_r1 - lf_mid.astype(F32)).astype(BF16)
        b3 = _dot(jnp.concatenate([lf_hi, lf_mid, lf_lo], axis=0), upper)
        b = b3[0:H, :] + b3[H:2 * H, :] + b3[2 * H:3 * H, :]
        r = ig - b
        cmax = r
        d = 1
        while d < L:
            cmax = jnp.maximum(cmax, jnp.where(lane >= d, pltpu.roll(cmax, d, 1), -jnp.inf))
            d *= 2
        m_run = jnp.maximum(cmax, m_prev)
        w_inter = jnp.exp(m_prev - m_run)
        e_negm = jnp.exp(-(b + m_run))
        m_last = m_run[:, L - 1:L]
        cols = jnp.concatenate([m_run, w_inter, e_negm, jnp.zeros((H, L), F32)], axis=0).T
        return dict(r=r, cols=cols, decay=w_inter[:, L - 1:L], wk=jnp.exp(r - m_last),
                    m_new=b[:, L - 1:L] + m_last)

    rq = []
    m_prev = m_ref[:, 0:1]
    for c in range(n_chunks):
        rq.append(chunk_rows(c, m_prev))
        m_prev = rq[c]["m_new"]
    m_ref[...] = jnp.broadcast_to(m_prev, (H, LANES))

    pair_lane = lax.broadcasted_iota(jnp.int32, (1, PW), 1)
    ones_blk = jnp.ones((L, M_DV), BF16)
    hnorm = hn_ref[...]

    def project(p):
        z = _dot(xn, wp_ref[p])
        q_pair = (z[:, 0:PW] * (M_DK ** -0.5)).astype(BF16)
        k_f32 = z[:, PW:2 * PW]
        v_pair = z[:, 2 * PW:2 * PW + 2 * M_DV].astype(BF16)
        og_pair = z[:, 2 * PW + 2 * M_DV:2 * PW + 4 * M_DV]
        return q_pair, k_f32, v_pair, og_pair

    def heads(p, proj, c):
        ts = slice(c * L, (c + 1) * L)
        q_pair, k_f32, v_pair, og_pair = (a[ts, :] for a in proj)
        k_pair = k_f32.astype(BF16)
        kt_pair = k_f32.T
        r, cols, decay, wk = rq[c]["r"], rq[c]["cols"], rq[c]["decay"], rq[c]["wk"]
        rows = slice(p * PW, (p + 1) * PW)
        c_pair = c_ref[rows, :].astype(BF16)
        hidx = (2 * p, 2 * p + 1)
        q_m, pmat, numden = [], [], []
        for j, h in enumerate(hidx):
            in_head = (pair_lane >= j * M_DK) & (pair_lane < (j + 1) * M_DK)
            q_m.append(jnp.where(in_head, q_pair, jnp.zeros_like(q_pair)))
        for j, h in enumerate(hidx):
            dmat = jnp.exp(jnp.where(causal, r[h:h + 1, :] - cols[:, h:h + 1], -jnp.inf))
            pmat.append((_dot_nt(q_m[j], k_pair) * dmat).astype(BF16))
        for j, h in enumerate(hidx):
            v_aug = jnp.concatenate([v_pair[:, j * M_DV:(j + 1) * M_DV], ones_blk], axis=1)
            qw = (cols[:, H + h:H + h + 1] * q_m[j].astype(F32)).astype(BF16)
            numden.append(_dot(jnp.concatenate([pmat[j], qw], axis=1),
                               jnp.concatenate([v_aug, c_pair], axis=0)))
            kw = (kt_pair[j * M_DK:(j + 1) * M_DK, :] * wk[h:h + 1, :]).astype(BF16)
            hs = slice(M_DK * h, M_DK * (h + 1))
            c_ref[hs, :] = decay[h:h + 1, :] * c_ref[hs, :] + _dot(kw, v_aug)
        for j, h in enumerate(hidx):
            vs = slice(M_DV * h, M_DV * (h + 1))
            den = numden[j][:, M_DV:2 * M_DV]
            hh = numden[j][:, 0:M_DV] * (1.0 / jnp.maximum(jnp.abs(den), cols[:, 2 * H + h:2 * H + h + 1]))
            hh = hh * lax.rsqrt(jnp.mean(hh * hh, axis=1, keepdims=True) + EPS)
            hh = hh * hnorm[:, vs]
            y_ref[ts, vs] = (jax.nn.sigmoid(og_pair[:, j * M_DV:(j + 1) * M_DV]) * hh).astype(BF16)

    n_pairs = H // 2
    proj = [project(0), project(1)]
    for p in range(n_pairs):
        for c in range(n_chunks):
            heads(p, proj[p], c)
        if p + 2 < n_pairs:
            proj.append(project(p + 2))

    o_ref[...] = x + _dot(y_ref[...], wout_ref[...])


def _mlstm(x2d, g, w_pairs, wgt, bcol, hnorm, w_out, layer, j):
    nt = x2d.shape[0] // BATCH // M_TM
    return pl.pallas_call(
        _mlstm_kernel,
        grid=(BATCH, nt),
        in_specs=[
            pl.BlockSpec((M_TM, D_MODEL), lambda b, s: (b * nt + s, 0)),
            _layer_spec((1, D_MODEL), layer, 2),
            _layer_spec((M_HEADS // 2, D_MODEL, M_PAIR_N), j, 2),
            _layer_spec((2 * M_HEADS, D_MODEL), j, 2),
            _layer_spec((2 * M_HEADS, 1), j, 2),
            _layer_spec((1, M_V), j, 2),
            _layer_spec((M_V, D_MODEL), j, 2),
        ],
        out_specs=pl.BlockSpec((M_TM, D_MODEL), lambda b, s: (b * nt + s, 0)),
        out_shape=jax.ShapeDtypeStruct(x2d.shape, F32),
        scratch_shapes=[
            pltpu.VMEM((M_HEADS * M_DK, 2 * M_DV), F32),
            pltpu.VMEM((M_HEADS, LANES), F32),
            pltpu.VMEM((M_TM, M_V), BF16),
        ],
        compiler_params=pltpu.CompilerParams(
            dimension_semantics=("parallel", "arbitrary"), vmem_limit_bytes=VMEM_LIMIT),
        name="mlstm",
    )(x2d, g, w_pairs, wgt, bcol, hnorm, w_out)


def _scan_time(a, bt, h0):
    nseg = R_TT // R_SEG
    rows = lambda t: slice(t * BATCH, (t + 1) * BATCH)
    ca = [[a[rows(k * R_SEG), :]] for k in range(nseg)]
    cb = [[bt[rows(k * R_SEG), :]] for k in range(nseg)]
    for i in range(1, R_SEG):
        for k in range(nseg):
            at = a[rows(k * R_SEG + i), :]
            cb[k].append(at * cb[k][i - 1] + bt[rows(k * R_SEG + i), :])
            ca[k].append(at * ca[k][i - 1])
    h_in = [h0]
    for k in range(nseg):
        h_in.append(ca[k][R_SEG - 1] * h_in[k] + cb[k][R_SEG - 1])
    hs = [ca[k][i] * h_in[k] + cb[k][i] for k in range(nseg) for i in range(R_SEG)]
    return jnp.concatenate(hs, axis=0), h_in[nseg]


def _rglru_kernel(x_ref, g_ref, win_ref, cw_ref, cb_ref, wai_ref, ba_ref, bi_ref, lam_ref,
                  wout_ref, o_ref, ubuf_ref, ucb_ref, h_ref):
    N = R_TT * BATCH
    HIST = (CONV_W - 1) * BATCH
    W = R_WIN_N
    nwin = R_WIDTH // W

    @pl.when(pl.program_id(0) == 0)
    def _():
        ubuf_ref[0:HIST, :] = jnp.zeros((HIST, R_WIDTH), F32)
        h_ref[...] = jnp.zeros_like(h_ref)

    x = x_ref[...].reshape(N, D_MODEL)
    xn = _rms_scale(x, g_ref[...]).astype(BF16)
    neg_c = -RG_C * _softplus(-lam_ref[...])

    def in_proj(c):
        cs = slice(c * W, (c + 1) * W)
        u = _dot(xn, win_ref[:, cs])
        gate = _dot(xn, win_ref[:, R_WIDTH + c * W:R_WIDTH + (c + 1) * W])
        ubuf_ref[HIST:HIST + N, cs] = u
        uc = cb_ref[:, cs] + cw_ref[CONV_W - 1:CONV_W, cs] * u
        for j in range(CONV_W - 1):
            uc = uc + cw_ref[j:j + 1, cs] * ubuf_ref[j * BATCH:j * BATCH + N, cs]
        ubuf_ref[0:HIST, cs] = ubuf_ref[N:N + HIST, cs]
        ucb_ref[:, cs] = uc.astype(BF16)
        return uc, gate

    def gate_proj(m):
        k0 = R_WIN_K0[m]
        gw = _dot(ucb_ref[:, k0:k0 + R_WIN_K], wai_ref[m])
        return gw[:, 0:W], gw[:, W:2 * W]

    def finish(m, uc, gate, pre, acc):
        cs = slice(m * W, (m + 1) * W)
        rg = jax.nn.sigmoid(pre[0] + ba_ref[:, cs])
        ig = jax.nn.sigmoid(pre[1] + bi_ref[:, cs])
        log_a = neg_c[:, cs] * rg
        a = jnp.exp(log_a)
        w = -jnp.tanh(log_a) * (a * a + 1.0)
        bt = (w * lax.rsqrt(jnp.maximum(w, F32_TINY))) * (ig * uc)
        h, h_last = _scan_time(a, bt, h_ref[:, cs])
        h_ref[:, cs] = h_last
        y = (h * jax.nn.gelu(gate, approximate=True)).astype(BF16)
        return acc + _dot(y, wout_ref[cs, :])

    pending = [in_proj(0), in_proj(1)]
    pre = gate_proj(0)
    acc = x
    for m in range(nwin):
        if m + 2 < nwin:
            pending.append(in_proj(m + 2))
        nxt = gate_proj(m + 1) if m + 1 < nwin else None
        acc = finish(m, pending[m][0], pending[m][1], pre, acc)
        pre = nxt
    o_ref[...] = acc.reshape(R_TT, BATCH, D_MODEL)


def _rglru(x3, g, w_in, conv_w, conv_b, w_ai, b_a, b_i, lam, w_out, layer, j):
    S = x3.shape[0]
    n_rows = R_TT * BATCH
    return pl.pallas_call(
        _rglru_kernel,
        grid=(S // R_TT,),
        in_specs=[
            pl.BlockSpec((R_TT, BATCH, D_MODEL), lambda s: (s, 0, 0)),
            _layer_spec((1, D_MODEL), layer, 1),
            _layer_spec((D_MODEL, 2 * R_WIDTH), j, 1),
            _layer_spec((CONV_W, R_WIDTH), j, 1),
            _layer_spec((1, R_WIDTH), j, 1),
            _layer_spec((len(R_WIN_K0), R_WIN_K, 2 * R_WIN_N), j, 1),
            _layer_spec((1, R_WIDTH), j, 1),
            _layer_spec((1, R_WIDTH), j, 1),
            _layer_spec((1, R_WIDTH), j, 1),
            _layer_spec((R_WIDTH, D_MODEL), j, 1),
        ],
        out_specs=pl.BlockSpec((R_TT, BATCH, D_MODEL), lambda s: (s, 0, 0)),
        out_shape=jax.ShapeDtypeStruct(x3.shape, F32),
        scratch_shapes=[
            pltpu.VMEM((n_rows + (CONV_W - 1) * BATCH, R_WIDTH), F32),
            pltpu.VMEM((n_rows, R_WIDTH), BF16),
            pltpu.VMEM((BATCH, R_WIDTH), F32),
        ],
        compiler_params=pltpu.CompilerParams(
            dimension_semantics=("arbitrary",), vmem_limit_bytes=VMEM_LIMIT),
        name="rglru",
    )(x3, g, w_in, conv_w, conv_b, w_ai, b_a, b_i, lam, w_out)


def _window_gates(w_a, w_i):
    nl = w_a.shape[0]
    eye = jnp.eye(R_BLOCKS, dtype=w_a.dtype)

    def dense(w):
        full = w[:, :, :, None, :] * eye[None, :, None, :, None]
        return full.reshape(nl, R_WIDTH, R_WIDTH)

    da, di = dense(w_a), dense(w_i)
    wins = []
    for m, k0 in enumerate(R_WIN_K0):
        cols = slice(m * R_WIN_N, (m + 1) * R_WIN_N)
        wins.append(jnp.concatenate([da[:, k0:k0 + R_WIN_K, cols], di[:, k0:k0 + R_WIN_K, cols]], axis=-1))
    return jnp.stack(wins, axis=1).astype(BF16)


def kernel(x, ff1_norm, ff1_w_in, ff1_w_out, mix_norm, ff2_norm, ff2_w_in, ff2_w_out, m_w_in, m_b_i, m_b_f, m_head_norm, m_w_out, r_w_in, r_conv_w, r_conv_b, r_w_a, r_b_a, r_w_i, r_b_i, r_lam, r_w_out, final_norm):
    B, S, D = x.shape
    depth = ff1_norm.shape[0]
    fg = final_norm.reshape(1, D)
    ff1_g, mix_g, ff2_g = (a.reshape(depth, 1, D) for a in (ff1_norm, mix_norm, ff2_norm))
    ff1_wi, ff1_wo, ff2_wi, ff2_wo = (a.astype(BF16) for a in (ff1_w_in, ff1_w_out, ff2_w_in, ff2_w_out))
    m_pairs = jnp.stack([jnp.concatenate([
        m_w_in[:, :, 2 * M_DK * p:2 * M_DK * (p + 1)],
        m_w_in[:, :, M_QK + 2 * M_DK * p:M_QK + 2 * M_DK * (p + 1)],
        m_w_in[:, :, 2 * M_QK + 2 * M_DV * p:2 * M_QK + 2 * M_DV * (p + 1)],
        m_w_in[:, :, 2 * M_QK + M_V + 2 * M_DV * p:2 * M_QK + M_V + 2 * M_DV * (p + 1)],
    ], axis=-1) for p in range(M_HEADS // 2)], axis=1).astype(BF16)
    m_wgt = jnp.swapaxes(m_w_in[:, :, M_N_MAIN:], 1, 2).astype(BF16)
    m_bcol = jnp.concatenate([m_b_i, m_b_f], axis=1)[:, :, None]
    m_hn = m_head_norm[:, None, :]
    m_wo = m_w_out.astype(BF16)
    r_wi = r_w_in.astype(BF16)
    r_cw = r_conv_w.reshape(-1, CONV_W, R_WIDTH)
    r_wai = _window_gates(r_w_a, r_w_i)
    r_cb, r_ba, r_bi, r_l = (a[:, None, :] for a in (r_conv_b, r_b_a, r_b_i, r_lam))
    r_wo = r_w_out.astype(BF16)

    xb = x
    for layer in range(depth):
        j = layer // 2
        if layer % 2 == 0:
            xb = _ffn(xb, False, False, ff1_g, ff1_wi, ff1_wo, layer, fg, False)
            xb = _mlstm(xb.reshape(B * S, D), mix_g, m_pairs, m_wgt, m_bcol, m_hn, m_wo,
                        layer, j).reshape(B, S, D)
            xb = _ffn(xb, False, False, ff2_g, ff2_wi, ff2_wo, layer, fg, layer == depth - 1)
        else:
            xt = _ffn(xb, False, True, ff1_g, ff1_wi, ff1_wo, layer, fg, False)
            xt = _rglru(xt, mix_g, r_wi, r_cw, r_cb, r_wai, r_ba, r_bi, r_l, r_wo, layer, j)
            xb = _ffn(xt, True, False, ff2_g, ff2_wi, ff2_wo, layer, fg, layer == depth - 1)
    return xb
```

```python
import functools

import jax
import jax.numpy as jnp
from jax import lax
from jax.experimental import pallas as pl
from jax.experimental.pallas import tpu as pltpu

D_MODEL = 1024
BATCH = 8
EPS = 1e-6
D_FF = 2816
M_HEADS = 8
M_QK = 512
M_V = 1024
M_DK = 64
M_DV = 128
R_WIDTH = 1280
R_BLOCKS = 8
R_BW = 160
CONV_W = 4
RG_C = 8.0

LANES = 128
SUBLANES = 8
BF16_SUBLANES = 16
VMEM_LIMIT = 56 * 1024 * 1024

FFN_TM = 1024
FFN_TT = FFN_TM // BATCH
FFN_TF = 256
FFN_TR = 128
FFN_TN = 256
FFN_CONV_ROWS = 176
M_L = 256
M_TM = 512
R_TT = 128
R_SEG = 8
R_WIN_N = 256
R_WIN_K = 512
R_WIN_K0 = tuple(min((m * R_WIN_N // R_BW) * R_BW // LANES * LANES, R_WIDTH - R_WIN_K)
                 for m in range(R_WIDTH // R_WIN_N))
for _m, _k0 in enumerate(R_WIN_K0):
    assert _k0 <= (_m * R_WIN_N // R_BW) * R_BW
    assert _k0 + R_WIN_K >= (((_m + 1) * R_WIN_N - 1) // R_BW + 1) * R_BW
M_N_MAIN = 2 * M_QK + 2 * M_V
M_PAIR_N = 4 * M_DK + 4 * M_DV

BF16 = jnp.bfloat16
F32 = jnp.float32
F32_TINY = float(jnp.finfo(jnp.float32).tiny)

assert BATCH == SUBLANES


def _rms_scale(x, g):
    ms = jnp.mean(x * x, axis=-1, keepdims=True)
    return x * lax.rsqrt(ms + EPS) * g


def _dot(a, b):
    return jnp.dot(a, b, preferred_element_type=F32)


def _dot_nt(a, b):
    return lax.dot_general(a, b, (((1,), (1,)), ((), ())), preferred_element_type=F32)


def _softplus(x):
    return jnp.maximum(x, 0.0) + jnp.log1p(jnp.exp(-jnp.abs(x)))


def _log_sigmoid(x):
    return jnp.minimum(x, 0.0) - jnp.log1p(jnp.exp(-jnp.abs(x)))


def _layer_spec(shape, layer, ngrid):
    zeros = (0,) * len(shape)
    if ngrid == 1:
        imap = lambda i: (layer,) + zeros
    else:
        imap = lambda i, j: (layer,) + zeros
    return pl.BlockSpec((None,) + shape, imap, pipeline_mode=pl.Buffered(1))


def _ffn_kernel(x_ref, g_ref, win_ref, wout_ref, fg_ref, *rest, final_norm, swap, convert_next):
    if convert_next:
        nin_ref, nout_ref, o_ref, nin_bf_ref, nout_bf_ref, act_ref = rest
        nin_bf_ref[...] = nin_ref[...].astype(BF16)
        nout_bf_ref[...] = nout_ref[...].astype(BF16)
    else:
        o_ref, act_ref = rest
    lead, per, _ = x_ref.shape
    nb = FFN_TR // per
    g = g_ref[...]

    def put(cols, y):
        y3 = y.reshape(lead, per, y.shape[-1])
        o_ref[:, :, cols] = jnp.swapaxes(y3, 0, 1) if swap else y3

    blocks = []
    for r in range(0, FFN_TM, FFN_TR):
        x_r = x_ref[r // per:r // per + nb].reshape(FFN_TR, D_MODEL)
        xn_r = _rms_scale(x_r, g).astype(BF16)
        hg = _dot(xn_r, win_ref[:, 0:FFN_TF])
        hu = _dot(xn_r, win_ref[:, D_FF:D_FF + FFN_TF])
        act_ref[r:r + FFN_TR, 0:FFN_TF] = (hg * jax.nn.sigmoid(hg) * hu).astype(BF16)
        blocks.append(xn_r)
    xn = jnp.concatenate(blocks, axis=0)
    for c in range(1, D_FF // FFN_TF):
        lo = c * FFN_TF
        hg = _dot(xn, win_ref[:, lo:lo + FFN_TF])
        hu = _dot(xn, win_ref[:, D_FF + lo:D_FF + lo + FFN_TF])
        act_ref[:, lo:lo + FFN_TF] = (hg * jax.nn.sigmoid(hg) * hu).astype(BF16)
    if final_norm:
        y = x_ref[...].reshape(FFN_TM, D_MODEL) + 0.5 * _dot(act_ref[...], wout_ref[...])
        put(slice(0, D_MODEL), _rms_scale(y, fg_ref[...]))
    else:
        for n in range(0, D_MODEL, FFN_TN):
            cols = slice(n, n + FFN_TN)
            put(cols, x_ref[:, :, cols].reshape(FFN_TM, FFN_TN) + 0.5 * _dot(act_ref[...], wout_ref[:, cols]))


def _ffn(x3, in_tm, out_tm, g, w_in, w_out, layer, final_g, final_norm, nxt=None):
    tm_shape, bm_shape = (FFN_TT, BATCH, D_MODEL), (BATCH, FFN_TT, D_MODEL)
    tm_map, bm_map = (lambda i: (i, 0, 0)), (lambda i: (0, i, 0))
    S = x3.shape[0] if in_tm else x3.shape[1]
    steps = S // FFN_TT
    const = lambda i: (0, 0)
    in_specs = [
        pl.BlockSpec(tm_shape if in_tm else bm_shape, tm_map if in_tm else bm_map),
        _layer_spec((1, D_MODEL), layer, 1),
        pl.BlockSpec((D_MODEL, 2 * D_FF), const),
        pl.BlockSpec((D_FF, D_MODEL), const),
        pl.BlockSpec((1, D_MODEL), const),
    ]
    out_specs = [pl.BlockSpec(tm_shape if out_tm else bm_shape, tm_map if out_tm else bm_map)]
    out_shape = [jax.ShapeDtypeStruct((S, BATCH, D_MODEL) if out_tm else (BATCH, S, D_MODEL), F32)]
    args = [x3, g, w_in, w_out, final_g]
    if nxt is not None:
        nin, nout, nl = nxt
        rin = D_MODEL // steps
        nslab = D_FF // FFN_CONV_ROWS
        assert D_MODEL % steps == 0 and rin % BF16_SUBLANES == 0 and nslab <= steps
        in_map = lambda i: (nl, i, 0)
        out_map = lambda i: (nl, jnp.minimum(i, nslab - 1), 0)
        in_specs += [pl.BlockSpec((None, rin, 2 * D_FF), in_map),
                     pl.BlockSpec((None, FFN_CONV_ROWS, D_MODEL), out_map)]
        out_specs += [pl.BlockSpec((rin, 2 * D_FF), lambda i: (i, 0)),
                      pl.BlockSpec((FFN_CONV_ROWS, D_MODEL), lambda i: (jnp.minimum(i, nslab - 1), 0))]
        out_shape += [jax.ShapeDtypeStruct((D_MODEL, 2 * D_FF), BF16),
                      jax.ShapeDtypeStruct((D_FF, D_MODEL), BF16)]
        args += [nin, nout]
    res = pl.pallas_call(
        functools.partial(_ffn_kernel, final_norm=final_norm, swap=in_tm != out_tm,
                          convert_next=nxt is not None),
        grid=(steps,),
        in_specs=in_specs,
        out_specs=out_specs,
        out_shape=out_shape,
        scratch_shapes=[pltpu.VMEM((FFN_TM, D_FF), BF16)],
        compiler_params=pltpu.CompilerParams(
            dimension_semantics=("arbitrary",), vmem_limit_bytes=VMEM_LIMIT),
        name="ffn",
    )(*args)
    return res[0] if nxt is None else res


def _mlstm_kernel(x_ref, g_ref, wp_ref, wgt_ref, bcol_ref, hn_ref, wout_ref,
                  o_ref, c_ref, m_ref, y_ref):
    L = M_L
    H = M_HEADS
    PW = 2 * M_DK
    n_chunks = M_TM // L

    @pl.when(pl.program_id(1) == 0)
    def _():
        c_ref[...] = jnp.zeros_like(c_ref)
        m_ref[...] = jnp.zeros_like(m_ref)

    x = x_ref[...]
    xn = _rms_scale(x, g_ref[...]).astype(BF16)

    gr = _dot_nt(wgt_ref[...], xn) + bcol_ref[...]
    row = lax.broadcasted_iota(jnp.int32, (L, L), 0)
    col = lax.broadcasted_iota(jnp.int32, (L, L), 1)
    causal = col <= row
    upper = (row <= col).astype(BF16)
    lane = lax.broadcasted_iota(jnp.int32, (H, L), 1)

    def chunk_rows(c, m_prev):
        ig = gr[0:H, c * L:(c + 1) * L]
        lf = _log_sigmoid(gr[H:2 * H, c * L:(c + 1) * L])
        lf_hi = lf.astype(BF16)
        lf_r1 = lf - lf_hi.astype(F32)
        lf_mid = lf_r1.astype(BF16)
        lf_lo = (lf_r1 - lf_mid.astype(F32)).astype(BF16)
        b3 = _dot(jnp.concatenate([lf_hi, lf_mid, lf_lo], axis=0), upper)
        b = b3[0:H, :] + b3[H:2 * H, :] + b3[2 * H:3 * H, :]
        r = ig - b
        cmax = r
        d = 1
        while d < L:
            cmax = jnp.maximum(cmax, jnp.where(lane >= d, pltpu.roll(cmax, d, 1), -jnp.inf))
            d *= 2
        m_run = jnp.maximum(cmax, m_prev)
        w_inter = jnp.exp(m_prev - m_run)
        e_negm = jnp.exp(-(b + m_run))
        m_last = m_run[:, L - 1:L]
        cols = jnp.concatenate([m_run, w_inter, e_negm, jnp.zeros((H, L), F32)], axis=0).T
        return dict(r=r, cols=cols, decay=w_inter[:, L - 1:L], wk=jnp.exp(r - m_last),
                    m_new=b[:, L - 1:L] + m_last)

    rq = []
    m_prev = m_ref[:, 0:1]
    for c in range(n_chunks):
        rq.append(chunk_rows(c, m_prev))
        m_prev = rq[c]["m_new"]
    m_ref[...] = jnp.broadcast_to(m_prev, (H, LANES))

    pair_lane = lax.broadcasted_iota(jnp.int32, (1, PW), 1)
    ones_blk = jnp.ones((L, M_DV), BF16)
    hnorm = hn_ref[...]

    def project(p):
        z = _dot(xn, wp_ref[p])
        q_pair = (z[:, 0:PW] * (M_DK ** -0.5)).astype(BF16)
        k_f32 = z[:, PW:2 * PW]
        v_pair = z[:, 2 * PW:2 * PW + 2 * M_DV].astype(BF16)
        og_pair = z[:, 2 * PW + 2 * M_DV:2 * PW + 4 * M_DV]
        return q_pair, k_f32, v_pair, og_pair

    def heads(p, proj, c):
        ts = slice(c * L, (c + 1) * L)
        q_pair, k_f32, v_pair, og_pair = (a[ts, :] for a in proj)
        k_pair = k_f32.astype(BF16)
        kt_pair = k_f32.T
        r, cols, decay, wk = rq[c]["r"], rq[c]["cols"], rq[c]["decay"], rq[c]["wk"]
        rows = slice(p * PW, (p + 1) * PW)
        c_pair = c_ref[rows, :].astype(BF16)
        hidx = (2 * p, 2 * p + 1)
        q_m, pmat, numden = [], [], []
        for j, h in enumerate(hidx):
            in_head = (pair_lane >= j * M_DK) & (pair_lane < (j + 1) * M_DK)
            q_m.append(jnp.where(in_head, q_pair, jnp.zeros_like(q_pair)))
        for j, h in enumerate(hidx):
            dmat = jnp.exp(jnp.where(causal, r[h:h + 1, :] - cols[:, h:h + 1], -jnp.inf))
            pmat.append((_dot_nt(q_m[j], k_pair) * dmat).astype(BF16))
        for j, h in enumerate(hidx):
            v_aug = jnp.concatenate([v_pair[:, j * M_DV:(j + 1) * M_DV], ones_blk], axis=1)
            qw = (cols[:, H + h:H + h + 1] * q_m[j].astype(F32)).astype(BF16)
            numden.append(_dot(jnp.concatenate([pmat[j], qw], axis=1),
                               jnp.concatenate([v_aug, c_pair], axis=0)))
            kw = (kt_pair[j * M_DK:(j + 1) * M_DK, :] * wk[h:h + 1, :]).astype(BF16)
            hs = slice(M_DK * h, M_DK * (h + 1))
            c_ref[hs, :] = decay[h:h + 1, :] * c_ref[hs, :] + _dot(kw, v_aug)
        for j, h in enumerate(hidx):
            vs = slice(M_DV * h, M_DV * (h + 1))
            den = numden[j][:, M_DV:2 * M_DV]
            hh = numden[j][:, 0:M_DV] * (1.0 / jnp.maximum(jnp.abs(den), cols[:, 2 * H + h:2 * H + h + 1]))
            hh = hh * lax.rsqrt(jnp.mean(hh * hh, axis=1, keepdims=True) + EPS)
            hh = hh * hnorm[:, vs]
            y_ref[ts, vs] = (jax.nn.sigmoid(og_pair[:, j * M_DV:(j + 1) * M_DV]) * hh).astype(BF16)

    n_pairs = H // 2
    proj = [project(0), project(1)]
    for p in range(n_pairs):
        for c in range(n_chunks):
            heads(p, proj[p], c)
        if p + 2 < n_pairs:
            proj.append(project(p + 2))

    o_ref[...] = x + _dot(y_ref[...], wout_ref[...])


def _mlstm(x2d, g, w_pairs, wgt, bcol, hnorm, w_out, layer, j):
    nt = x2d.shape[0] // BATCH // M_TM
    return pl.pallas_call(
        _mlstm_kernel,
        grid=(BATCH, nt),
        in_specs=[
            pl.BlockSpec((M_TM, D_MODEL), lambda b, s: (b * nt + s, 0)),
            _layer_spec((1, D_MODEL), layer, 2),
            _layer_spec((M_HEADS // 2, D_MODEL, M_PAIR_N), j, 2),
            _layer_spec((2 * M_HEADS, D_MODEL), j, 2),
            _layer_spec((2 * M_HEADS, 1), j, 2),
            _layer_spec((1, M_V), j, 2),
            _layer_spec((M_V, D_MODEL), j, 2),
        ],
        out_specs=pl.BlockSpec((M_TM, D_MODEL), lambda b, s: (b * nt + s, 0)),
        out_shape=jax.ShapeDtypeStruct(x2d.shape, F32),
        scratch_shapes=[
            pltpu.VMEM((M_HEADS * M_DK, 2 * M_DV), F32),
            pltpu.VMEM((M_HEADS, LANES), F32),
            pltpu.VMEM((M_TM, M_V), BF16),
        ],
        compiler_params=pltpu.CompilerParams(
            dimension_semantics=("parallel", "arbitrary"), vmem_limit_bytes=VMEM_LIMIT),
        name="mlstm",
    )(x2d, g, w_pairs, wgt, bcol, hnorm, w_out)


def _scan_time(a, bt, h0):
    nseg = R_TT // R_SEG
    rows = lambda t: slice(t * BATCH, (t + 1) * BATCH)
    ca = [[a[rows(k * R_SEG), :]] for k in range(nseg)]
    cb = [[bt[rows(k * R_SEG), :]] for k in range(nseg)]
    for i in range(1, R_SEG):
        for k in range(nseg):
            at = a[rows(k * R_SEG + i), :]
            cb[k].append(at * cb[k][i - 1] + bt[rows(k * R_SEG + i), :])
            ca[k].append(at * ca[k][i - 1])
    h_in = [h0]
    for k in range(nseg):
        h_in.append(ca[k][R_SEG - 1] * h_in[k] + cb[k][R_SEG - 1])
    hs = [ca[k][i] * h_in[k] + cb[k][i] for k in range(nseg) for i in range(R_SEG)]
    return jnp.concatenate(hs, axis=0), h_in[nseg]


def _rglru_kernel(x_ref, g_ref, win_ref, cw_ref, cb_ref, wai_ref, ba_ref, bi_ref, lam_ref,
                  wout_ref, o_ref, ubuf_ref, ucb_ref, h_ref):
    N = R_TT * BATCH
    HIST = (CONV_W - 1) * BATCH
    W = R_WIN_N
    nwin = R_WIDTH // W

    @pl.when(pl.program_id(0) == 0)
    def _():
        ubuf_ref[0:HIST, :] = jnp.zeros((HIST, R_WIDTH), F32)
        h_ref[...] = jnp.zeros_like(h_ref)

    x = x_ref[...].reshape(N, D_MODEL)
    xn = _rms_scale(x, g_ref[...]).astype(BF16)
    neg_c = -RG_C * _softplus(-lam_ref[...])

    def in_proj(c):
        cs = slice(c * W, (c + 1) * W)
        u = _dot(xn, win_ref[:, cs])
        gate = _dot(xn, win_ref[:, R_WIDTH + c * W:R_WIDTH + (c + 1) * W])
        ubuf_ref[HIST:HIST + N, cs] = u
        uc = cb_ref[:, cs] + cw_ref[CONV_W - 1:CONV_W, cs] * u
        for j in range(CONV_W - 1):
            uc = uc + cw_ref[j:j + 1, cs] * ubuf_ref[j * BATCH:j * BATCH + N, cs]
        ubuf_ref[0:HIST, cs] = ubuf_ref[N:N + HIST, cs]
        ucb_ref[:, cs] = uc.astype(BF16)
        return uc, gate

    def gate_proj(m):
        k0 = R_WIN_K0[m]
        gw = _dot(ucb_ref[:, k0:k0 + R_WIN_K], wai_ref[m])
        return gw[:, 0:W], gw[:, W:2 * W]

    def finish(m, uc, gate, pre, acc):
        cs = slice(m * W, (m + 1) * W)
        rg = jax.nn.sigmoid(pre[0] + ba_ref[:, cs])
        ig = jax.nn.sigmoid(pre[1] + bi_ref[:, cs])
        log_a = neg_c[:, cs] * rg
        a = jnp.exp(log_a)
        w = -jnp.tanh(log_a) * (a * a + 1.0)
        bt = (w * lax.rsqrt(jnp.maximum(w, F32_TINY))) * (ig * uc)
        h, h_last = _scan_time(a, bt, h_ref[:, cs])
        h_ref[:, cs] = h_last
        y = (h * jax.nn.gelu(gate, approximate=True)).astype(BF16)
        return acc + _dot(y, wout_ref[cs, :])

    pending = [in_proj(0), in_proj(1)]
    pre = gate_proj(0)
    acc = x
    for m in range(nwin):
        if m + 2 < nwin:
            pending.append(in_proj(m + 2))
        nxt = gate_proj(m + 1) if m + 1 < nwin else None
        acc = finish(m, pending[m][0], pending[m][1], pre, acc)
        pre = nxt
    o_ref[...] = acc.reshape(R_TT, BATCH, D_MODEL)


def _rglru(x3, g, w_in, conv_w, conv_b, w_ai, b_a, b_i, lam, w_out, layer, j):
    S = x3.shape[0]
    n_rows = R_TT * BATCH
    return pl.pallas_call(
        _rglru_kernel,
        grid=(S // R_TT,),
        in_specs=[
            pl.BlockSpec((R_TT, BATCH, D_MODEL), lambda s: (s, 0, 0)),
            _layer_spec((1, D_MODEL), layer, 1),
            _layer_spec((D_MODEL, 2 * R_WIDTH), j, 1),
            _layer_spec((CONV_W, R_WIDTH), j, 1),
            _layer_spec((1, R_WIDTH), j, 1),
            _layer_spec((len(R_WIN_K0), R_WIN_K, 2 * R_WIN_N), j, 1),
            _layer_spec((1, R_WIDTH), j, 1),
            _layer_spec((1, R_WIDTH), j, 1),
            _layer_spec((1, R_WIDTH), j, 1),
            _layer_spec((R_WIDTH, D_MODEL), j, 1),
        ],
        out_specs=pl.BlockSpec((R_TT, BATCH, D_MODEL), lambda s: (s, 0, 0)),
        out_shape=jax.ShapeDtypeStruct(x3.shape, F32),
        scratch_shapes=[
            pltpu.VMEM((n_rows + (CONV_W - 1) * BATCH, R_WIDTH), F32),
            pltpu.VMEM((n_rows, R_WIDTH), BF16),
            pltpu.VMEM((BATCH, R_WIDTH), F32),
        ],
        compiler_params=pltpu.CompilerParams(
            dimension_semantics=("arbitrary",), vmem_limit_bytes=VMEM_LIMIT),
        name="rglru",
    )(x3, g, w_in, conv_w, conv_b, w_ai, b_a, b_i, lam, w_out)


def _window_gates(w_a, w_i):
    wins = []
    for m, k0 in enumerate(R_WIN_K0):
        c0 = m * R_WIN_N
        halves = []
        for w in (w_a, w_i):
            acc = None
            for n in range(R_BLOCKS):
                lo, hi = max(n * R_BW, c0), min((n + 1) * R_BW, c0 + R_WIN_N)
                if lo >= hi:
                    continue
                piece = w[:, n, :, lo - n * R_BW:hi - n * R_BW]
                r0 = n * R_BW - k0
                padded = jnp.pad(piece, ((0, 0), (r0, R_WIN_K - r0 - R_BW), (lo - c0, c0 + R_WIN_N - hi)))
                acc = padded if acc is None else acc + padded
            halves.append(acc)
        wins.append(jnp.concatenate(halves, axis=-1))
    return jnp.stack(wins, axis=1).astype(BF16)


def kernel(x, ff1_norm, ff1_w_in, ff1_w_out, mix_norm, ff2_norm, ff2_w_in, ff2_w_out, m_w_in, m_b_i, m_b_f, m_head_norm, m_w_out, r_w_in, r_conv_w, r_conv_b, r_w_a, r_b_a, r_w_i, r_b_i, r_lam, r_w_out, final_norm):
    B, S, D = x.shape
    depth = ff1_norm.shape[0]
    fg = final_norm.reshape(1, D)
    ff1_g, mix_g, ff2_g = (a.reshape(depth, 1, D) for a in (ff1_norm, mix_norm, ff2_norm))
    ffn_w = [(ff1_w_in, ff1_w_out), (ff2_w_in, ff2_w_out)]
    w_cur = (ff1_w_in[0].astype(BF16), ff1_w_out[0].astype(BF16))
    m_pairs = jnp.stack([jnp.concatenate([
        m_w_in[:, :, 2 * M_DK * p:2 * M_DK * (p + 1)],
        m_w_in[:, :, M_QK + 2 * M_DK * p:M_QK + 2 * M_DK * (p + 1)],
        m_w_in[:, :, 2 * M_QK + 2 * M_DV * p:2 * M_QK + 2 * M_DV * (p + 1)],
        m_w_in[:, :, 2 * M_QK + M_V + 2 * M_DV * p:2 * M_QK + M_V + 2 * M_DV * (p + 1)],
    ], axis=-1) for p in range(M_HEADS // 2)], axis=1).astype(BF16)
    m_wgt = jnp.swapaxes(m_w_in[:, :, M_N_MAIN:], 1, 2).astype(BF16)
    m_bcol = jnp.concatenate([m_b_i, m_b_f], axis=1)[:, :, None]
    m_hn = m_head_norm[:, None, :]
    m_wo = m_w_out.astype(BF16)
    r_wi = r_w_in.astype(BF16)
    r_cw = r_conv_w.reshape(-1, CONV_W, R_WIDTH)
    r_wai = _window_gates(r_w_a, r_w_i)
    r_cb, r_ba, r_bi, r_l = (a[:, None, :] for a in (r_conv_b, r_b_a, r_b_i, r_lam))
    r_wo = r_w_out.astype(BF16)

    def ffn(xin, in_tm, out_tm, which, layer, w_cur):
        g = (ff1_g, ff2_g)[which]
        last = which == 1 and layer == depth - 1
        if last:
            return _ffn(xin, in_tm, out_tm, g, w_cur[0], w_cur[1], layer, fg, True), None
        nw = ffn_w[1 - which]
        out, nin, nout = _ffn(xin, in_tm, out_tm, g, w_cur[0], w_cur[1], layer, fg, False,
                              nxt=(nw[0], nw[1], layer + which))
        return out, (nin, nout)

    xb = x
    for layer in range(depth):
        j = layer // 2
        if layer % 2 == 0:
            xb, w_cur = ffn(xb, False, False, 0, layer, w_cur)
            xb = _mlstm(xb.reshape(B * S, D), mix_g, m_pairs, m_wgt, m_bcol, m_hn, m_wo,
                        layer, j).reshape(B, S, D)
            xb, w_cur = ffn(xb, False, False, 1, layer, w_cur)
        else:
            xt, w_cur = ffn(xb, False, True, 0, layer, w_cur)
            xt = _rglru(xt, mix_g, r_wi, r_cw, r_cb, r_wai, r_ba, r_bi, r_l, r_wo, layer, j)
            xb, w_cur = ffn(xt, True, False, 1, layer, w_cur)
    return xb
```

```python
import functools

import jax
import jax.numpy as jnp
from jax import lax
from jax.experimental import pallas as pl
from jax.experimental.pallas import tpu as pltpu

D_MODEL = 1024
BATCH = 8
EPS = 1e-6
D_FF = 2816
M_HEADS = 8
M_QK = 512
M_V = 1024
M_DK = 64
M_DV = 128
R_WIDTH = 1280
R_BLOCKS = 8
R_BW = 160
CONV_W = 4
RG_C = 8.0

LANES = 128
SUBLANES = 8
BF16_SUBLANES = 16
VMEM_LIMIT = 56 * 1024 * 1024

FFN_TM = 1024
FFN_TT = FFN_TM // BATCH
FFN_TF = 256
FFN_TR = 128
FFN_TN = 256
FFN_CONV_ROWS = 176
M_L = 256
M_TM = 1024
R_TT = 128
R_SEG = 8
R_OUT_GROUP = 4
R_WIN_N = 256
R_WIN_K = 512
R_WIN_K0 = tuple(min((m * R_WIN_N // R_BW) * R_BW // LANES * LANES, R_WIDTH - R_WIN_K)
                 for m in range(R_WIDTH // R_WIN_N))
for _m, _k0 in enumerate(R_WIN_K0):
    assert _k0 <= (_m * R_WIN_N // R_BW) * R_BW
    assert _k0 + R_WIN_K >= (((_m + 1) * R_WIN_N - 1) // R_BW + 1) * R_BW
M_N_MAIN = 2 * M_QK + 2 * M_V
M_PAIR_N = 4 * M_DK + 4 * M_DV

BF16 = jnp.bfloat16
F32 = jnp.float32
F32_TINY = float(jnp.finfo(jnp.float32).tiny)

assert BATCH == SUBLANES


def _rms_scale(x, g):
    ms = jnp.mean(x * x, axis=-1, keepdims=True)
    return x * lax.rsqrt(ms + EPS) * g


def _dot(a, b):
    return jnp.dot(a, b, preferred_element_type=F32)


def _dot_nt(a, b):
    return lax.dot_general(a, b, (((1,), (1,)), ((), ())), preferred_element_type=F32)


def _softplus(x):
    return jnp.maximum(x, 0.0) + jnp.log1p(jnp.exp(-jnp.abs(x)))


def _log_sigmoid(x):
    return jnp.minimum(x, 0.0) - jnp.log1p(jnp.exp(-jnp.abs(x)))


def _layer_spec(shape, layer, ngrid):
    zeros = (0,) * len(shape)
    if ngrid == 1:
        imap = lambda i: (layer,) + zeros
    else:
        imap = lambda i, j: (layer,) + zeros
    return pl.BlockSpec((None,) + shape, imap, pipeline_mode=pl.Buffered(1))


def _ffn_kernel(x_ref, g_ref, win_ref, wout_ref, fg_ref, *rest, final_norm, swap, convert_next):
    if convert_next:
        nin_ref, nout_ref, o_ref, nin_bf_ref, nout_bf_ref, act_ref = rest
        nin_bf_ref[...] = nin_ref[...].astype(BF16)
        nout_bf_ref[...] = nout_ref[...].astype(BF16)
    else:
        o_ref, act_ref = rest
    lead, per, _ = x_ref.shape
    nb = FFN_TR // per
    g = g_ref[...]

    def put(cols, y):
        y3 = y.reshape(lead, per, y.shape[-1])
        o_ref[:, :, cols] = jnp.swapaxes(y3, 0, 1) if swap else y3

    blocks = []
    for r in range(0, FFN_TM, FFN_TR):
        x_r = x_ref[r // per:r // per + nb].reshape(FFN_TR, D_MODEL)
        xn_r = _rms_scale(x_r, g).astype(BF16)
        hg = _dot(xn_r, win_ref[:, 0:FFN_TF])
        hu = _dot(xn_r, win_ref[:, D_FF:D_FF + FFN_TF])
        act_ref[r:r + FFN_TR, 0:FFN_TF] = (hg * jax.nn.sigmoid(hg) * hu).astype(BF16)
        blocks.append(xn_r)
    xn = jnp.concatenate(blocks, axis=0)
    for c in range(1, D_FF // FFN_TF):
        lo = c * FFN_TF
        hg = _dot(xn, win_ref[:, lo:lo + FFN_TF])
        hu = _dot(xn, win_ref[:, D_FF + lo:D_FF + lo + FFN_TF])
        act_ref[:, lo:lo + FFN_TF] = (hg * jax.nn.sigmoid(hg) * hu).astype(BF16)
    if final_norm:
        y = x_ref[...].reshape(FFN_TM, D_MODEL) + 0.5 * _dot(act_ref[...], wout_ref[...])
        put(slice(0, D_MODEL), _rms_scale(y, fg_ref[...]))
    else:
        for n in range(0, D_MODEL, FFN_TN):
            cols = slice(n, n + FFN_TN)
            put(cols, x_ref[:, :, cols].reshape(FFN_TM, FFN_TN) + 0.5 * _dot(act_ref[...], wout_ref[:, cols]))


def _ffn(x3, in_tm, out_tm, g, w_in, w_out, layer, final_g, final_norm, nxt=None):
    tm_shape, bm_shape = (FFN_TT, BATCH, D_MODEL), (BATCH, FFN_TT, D_MODEL)
    tm_map, bm_map = (lambda i: (i, 0, 0)), (lambda i: (0, i, 0))
    S = x3.shape[0] if in_tm else x3.shape[1]
    steps = S // FFN_TT
    const = lambda i: (0, 0)
    in_specs = [
        pl.BlockSpec(tm_shape if in_tm else bm_shape, tm_map if in_tm else bm_map),
        _layer_spec((1, D_MODEL), layer, 1),
        pl.BlockSpec((D_MODEL, 2 * D_FF), const),
        pl.BlockSpec((D_FF, D_MODEL), const),
        pl.BlockSpec((1, D_MODEL), const),
    ]
    out_specs = [pl.BlockSpec(tm_shape if out_tm else bm_shape, tm_map if out_tm else bm_map)]
    out_shape = [jax.ShapeDtypeStruct((S, BATCH, D_MODEL) if out_tm else (BATCH, S, D_MODEL), F32)]
    args = [x3, g, w_in, w_out, final_g]
    if nxt is not None:
        nin, nout, nl = nxt
        rin = D_MODEL // steps
        nslab = D_FF // FFN_CONV_ROWS
        assert D_MODEL % steps == 0 and rin % BF16_SUBLANES == 0 and nslab <= steps
        in_map = lambda i: (nl, i, 0)
        out_map = lambda i: (nl, jnp.minimum(i, nslab - 1), 0)
        in_specs += [pl.BlockSpec((None, rin, 2 * D_FF), in_map),
                     pl.BlockSpec((None, FFN_CONV_ROWS, D_MODEL), out_map)]
        out_specs += [pl.BlockSpec((rin, 2 * D_FF), lambda i: (i, 0)),
                      pl.BlockSpec((FFN_CONV_ROWS, D_MODEL), lambda i: (jnp.minimum(i, nslab - 1), 0))]
        out_shape += [jax.ShapeDtypeStruct((D_MODEL, 2 * D_FF), BF16),
                      jax.ShapeDtypeStruct((D_FF, D_MODEL), BF16)]
        args += [nin, nout]
    res = pl.pallas_call(
        functools.partial(_ffn_kernel, final_norm=final_norm, swap=in_tm != out_tm,
                          convert_next=nxt is not None),
        grid=(steps,),
        in_specs=in_specs,
        out_specs=out_specs,
        out_shape=out_shape,
        scratch_shapes=[pltpu.VMEM((FFN_TM, D_FF), BF16)],
        compiler_params=pltpu.CompilerParams(
            dimension_semantics=("arbitrary",), vmem_limit_bytes=VMEM_LIMIT),
        name="ffn",
    )(*args)
    return res[0] if nxt is None else res


def _mlstm_kernel(x_ref, g_ref, wp_ref, wgt_ref, bcol_ref, hn_ref, wout_ref,
                  o_ref, c_ref, m_ref, y_ref):
    L = M_L
    H = M_HEADS
    PW = 2 * M_DK
    n_chunks = M_TM // L

    @pl.when(pl.program_id(1) == 0)
    def _():
        c_ref[...] = jnp.zeros_like(c_ref)
        m_ref[...] = jnp.zeros_like(m_ref)

    x = x_ref[...]
    xn = _rms_scale(x, g_ref[...]).astype(BF16)

    gr = _dot_nt(wgt_ref[...], xn) + bcol_ref[...]
    row = lax.broadcasted_iota(jnp.int32, (L, L), 0)
    col = lax.broadcasted_iota(jnp.int32, (L, L), 1)
    causal = col <= row
    upper = (row <= col).astype(BF16)
    lane = lax.broadcasted_iota(jnp.int32, (H, L), 1)

    def chunk_rows(c, m_prev):
        ig = gr[0:H, c * L:(c + 1) * L]
        lf = _log_sigmoid(gr[H:2 * H, c * L:(c + 1) * L])
        lf_hi = lf.astype(BF16)
        lf_r1 = lf - lf_hi.astype(F32)
        lf_mid = lf_r1.astype(BF16)
        lf_lo = (lf_r1 - lf_mid.astype(F32)).astype(BF16)
        b3 = _dot(jnp.concatenate([lf_hi, lf_mid, lf_lo], axis=0), upper)
        b = b3[0:H, :] + b3[H:2 * H, :] + b3[2 * H:3 * H, :]
        r = ig - b
        cmax = r
        d = 1
        while d < L:
            cmax = jnp.maximum(cmax, jnp.where(lane >= d, pltpu.roll(cmax, d, 1), -jnp.inf))
            d *= 2
        m_run = jnp.maximum(cmax, m_prev)
        w_inter = jnp.exp(m_prev - m_run)
        e_negm = jnp.exp(-(b + m_run))
        m_last = m_run[:, L - 1:L]
        cols = jnp.concatenate([m_run, w_inter, e_negm, jnp.zeros((H, L), F32)], axis=0).T
        return dict(r=r, cols=cols, decay=w_inter[:, L - 1:L], wk=jnp.exp(r - m_last),
                    m_new=b[:, L - 1:L] + m_last)

    rq = []
    m_prev = m_ref[:, 0:1]
    for c in range(n_chunks):
        rq.append(chunk_rows(c, m_prev))
        m_prev = rq[c]["m_new"]
    m_ref[...] = jnp.broadcast_to(m_prev, (H, LANES))

    pair_lane = lax.broadcasted_iota(jnp.int32, (1, PW), 1)
    ones_blk = jnp.ones((L, M_DV), BF16)
    hnorm = hn_ref[...]

    def project(p):
        z = _dot(xn, wp_ref[p])
        q_pair = (z[:, 0:PW] * (M_DK ** -0.5)).astype(BF16)
        k_f32 = z[:, PW:2 * PW]
        v_pair = z[:, 2 * PW:2 * PW + 2 * M_DV].astype(BF16)
        og_pair = z[:, 2 * PW + 2 * M_DV:2 * PW + 4 * M_DV]
        return q_pair, k_f32, v_pair, og_pair

    def heads(p, proj, c):
        ts = slice(c * L, (c + 1) * L)
        q_pair, k_f32, v_pair, og_pair = (a[ts, :] for a in proj)
        k_pair = k_f32.astype(BF16)
        kt_pair = k_f32.T
        r, cols, decay, wk = rq[c]["r"], rq[c]["cols"], rq[c]["decay"], rq[c]["wk"]
        rows = slice(p * PW, (p + 1) * PW)
        c_pair = c_ref[rows, :].astype(BF16)
        hidx = (2 * p, 2 * p + 1)
        q_m, pmat, numden = [], [], []
        for j, h in enumerate(hidx):
            in_head = (pair_lane >= j * M_DK) & (pair_lane < (j + 1) * M_DK)
            q_m.append(jnp.where(in_head, q_pair, jnp.zeros_like(q_pair)))
        for j, h in enumerate(hidx):
            dmat = jnp.exp(jnp.where(causal, r[h:h + 1, :] - cols[:, h:h + 1], -jnp.inf))
            pmat.append((_dot_nt(q_m[j], k_pair) * dmat).astype(BF16))
        for j, h in enumerate(hidx):
            v_aug = jnp.concatenate([v_pair[:, j * M_DV:(j + 1) * M_DV], ones_blk], axis=1)
            qw = (cols[:, H + h:H + h + 1] * q_m[j].astype(F32)).astype(BF16)
            numden.append(_dot(jnp.concatenate([pmat[j], qw], axis=1),
                               jnp.concatenate([v_aug, c_pair], axis=0)))
            kw = (kt_pair[j * M_DK:(j + 1) * M_DK, :] * wk[h:h + 1, :]).astype(BF16)
            hs = slice(M_DK * h, M_DK * (h + 1))
            c_ref[hs, :] = decay[h:h + 1, :] * c_ref[hs, :] + _dot(kw, v_aug)
        for j, h in enumerate(hidx):
            vs = slice(M_DV * h, M_DV * (h + 1))
            den = numden[j][:, M_DV:2 * M_DV]
            hh = numden[j][:, 0:M_DV] * (1.0 / jnp.maximum(jnp.abs(den), cols[:, 2 * H + h:2 * H + h + 1]))
            hh = hh * lax.rsqrt(jnp.mean(hh * hh, axis=1, keepdims=True) + EPS)
            hh = hh * hnorm[:, vs]
            y_ref[ts, vs] = (jax.nn.sigmoid(og_pair[:, j * M_DV:(j + 1) * M_DV]) * hh).astype(BF16)

    n_pairs = H // 2
    proj = [project(0), project(1)]
    for p in range(n_pairs):
        for c in range(n_chunks):
            heads(p, proj[p], c)
        if p + 2 < n_pairs:
            proj.append(project(p + 2))

    o_ref[...] = x + _dot(y_ref[...], wout_ref[...])


def _mlstm(x2d, g, w_pairs, wgt, bcol, hnorm, w_out, layer, j):
    nt = x2d.shape[0] // BATCH // M_TM
    return pl.pallas_call(
        _mlstm_kernel,
        grid=(BATCH, nt),
        in_specs=[
            pl.BlockSpec((M_TM, D_MODEL), lambda b, s: (b * nt + s, 0)),
            _layer_spec((1, D_MODEL), layer, 2),
            _layer_spec((M_HEADS // 2, D_MODEL, M_PAIR_N), j, 2),
            _layer_spec((2 * M_HEADS, D_MODEL), j, 2),
            _layer_spec((2 * M_HEADS, 1), j, 2),
            _layer_spec((1, M_V), j, 2),
            _layer_spec((M_V, D_MODEL), j, 2),
        ],
        out_specs=pl.BlockSpec((M_TM, D_MODEL), lambda b, s: (b * nt + s, 0)),
        out_shape=jax.ShapeDtypeStruct(x2d.shape, F32),
        scratch_shapes=[
            pltpu.VMEM((M_HEADS * M_DK, 2 * M_DV), F32),
            pltpu.VMEM((M_HEADS, LANES), F32),
            pltpu.VMEM((M_TM, M_V), BF16),
        ],
        compiler_params=pltpu.CompilerParams(
            dimension_semantics=("parallel", "arbitrary"), vmem_limit_bytes=VMEM_LIMIT),
        name="mlstm",
    )(x2d, g, w_pairs, wgt, bcol, hnorm, w_out)


def _scan_time(a, bt, h0):
    nseg = R_TT // R_SEG
    rows = lambda t: slice(t * BATCH, (t + 1) * BATCH)
    ca = [[a[rows(k * R_SEG), :]] for k in range(nseg)]
    cb = [[bt[rows(k * R_SEG), :]] for k in range(nseg)]
    for i in range(1, R_SEG):
        for k in range(nseg):
            at = a[rows(k * R_SEG + i), :]
            cb[k].append(at * cb[k][i - 1] + bt[rows(k * R_SEG + i), :])
            ca[k].append(at * ca[k][i - 1])
    h_in = [h0]
    for k in range(nseg):
        h_in.append(ca[k][R_SEG - 1] * h_in[k] + cb[k][R_SEG - 1])
    hs = [ca[k][i] * h_in[k] + cb[k][i] for k in range(nseg) for i in range(R_SEG)]
    return jnp.concatenate(hs, axis=0), h_in[nseg]


def _rglru_kernel(x_ref, g_ref, win_ref, cw_ref, cb_ref, wai_ref, ba_ref, bi_ref, lam_ref,
                  wout_ref, o_ref, ubuf_ref, ucb_ref, h_ref):
    N = R_TT * BATCH
    HIST = (CONV_W - 1) * BATCH
    W = R_WIN_N
    nwin = R_WIDTH // W

    @pl.when(pl.program_id(0) == 0)
    def _():
        ubuf_ref[0:HIST, :] = jnp.zeros((HIST, R_WIDTH), F32)
        h_ref[...] = jnp.zeros_like(h_ref)

    x = x_ref[...].reshape(N, D_MODEL)
    xn = _rms_scale(x, g_ref[...]).astype(BF16)
    neg_c = -RG_C * _softplus(-lam_ref[...])

    def in_proj(c):
        cs = slice(c * W, (c + 1) * W)
        u = _dot(xn, win_ref[:, cs])
        gate = _dot(xn, win_ref[:, R_WIDTH + c * W:R_WIDTH + (c + 1) * W])
        ubuf_ref[HIST:HIST + N, cs] = u
        uc = cb_ref[:, cs] + cw_ref[CONV_W - 1:CONV_W, cs] * u
        for j in range(CONV_W - 1):
            uc = uc + cw_ref[j:j + 1, cs] * ubuf_ref[j * BATCH:j * BATCH + N, cs]
        ubuf_ref[0:HIST, cs] = ubuf_ref[N:N + HIST, cs]
        ucb_ref[:, cs] = uc.astype(BF16)
        return uc, gate

    def gate_proj(m):
        k0 = R_WIN_K0[m]
        gw = _dot(ucb_ref[:, k0:k0 + R_WIN_K], wai_ref[m])
        return gw[:, 0:W], gw[:, W:2 * W]

    def finish(m, uc, gate, pre):
        cs = slice(m * W, (m + 1) * W)
        rg = jax.nn.sigmoid(pre[0] + ba_ref[:, cs])
        ig = jax.nn.sigmoid(pre[1] + bi_ref[:, cs])
        log_a = neg_c[:, cs] * rg
        a = jnp.exp(log_a)
        w = -jnp.tanh(log_a) * (a * a + 1.0)
        bt = (w * lax.rsqrt(jnp.maximum(w, F32_TINY))) * (ig * uc)
        h, h_last = _scan_time(a, bt, h_ref[:, cs])
        h_ref[:, cs] = h_last
        return (h * jax.nn.gelu(gate, approximate=True)).astype(BF16)

    pending = [in_proj(0), in_proj(1)]
    pre = gate_proj(0)
    acc = x
    ys = []
    for m in range(nwin):
        if m + 2 < nwin:
            pending.append(in_proj(m + 2))
        nxt = gate_proj(m + 1) if m + 1 < nwin else None
        ys.append(finish(m, pending[m][0], pending[m][1], pre))
        pre = nxt
        if len(ys) == R_OUT_GROUP or m == nwin - 1:
            lo = (m + 1 - len(ys)) * W
            acc = acc + _dot(jnp.concatenate(ys, axis=1), wout_ref[lo:(m + 1) * W, :])
            ys = []
    o_ref[...] = acc.reshape(R_TT, BATCH, D_MODEL)


def _rglru(x3, g, w_in, conv_w, conv_b, w_ai, b_a, b_i, lam, w_out, layer, j):
    S = x3.shape[0]
    n_rows = R_TT * BATCH
    return pl.pallas_call(
        _rglru_kernel,
        grid=(S // R_TT,),
        in_specs=[
            pl.BlockSpec((R_TT, BATCH, D_MODEL), lambda s: (s, 0, 0)),
            _layer_spec((1, D_MODEL), layer, 1),
            _layer_spec((D_MODEL, 2 * R_WIDTH), j, 1),
            _layer_spec((CONV_W, R_WIDTH), j, 1),
            _layer_spec((1, R_WIDTH), j, 1),
            _layer_spec((len(R_WIN_K0), R_WIN_K, 2 * R_WIN_N), j, 1),
            _layer_spec((1, R_WIDTH), j, 1),
            _layer_spec((1, R_WIDTH), j, 1),
            _layer_spec((1, R_WIDTH), j, 1),
            _layer_spec((R_WIDTH, D_MODEL), j, 1),
        ],
        out_specs=pl.BlockSpec((R_TT, BATCH, D_MODEL), lambda s: (s, 0, 0)),
        out_shape=jax.ShapeDtypeStruct(x3.shape, F32),
        scratch_shapes=[
            pltpu.VMEM((n_rows + (CONV_W - 1) * BATCH, R_WIDTH), F32),
            pltpu.VMEM((n_rows, R_WIDTH), BF16),
            pltpu.VMEM((BATCH, R_WIDTH), F32),
        ],
        compiler_params=pltpu.CompilerParams(
            dimension_semantics=("arbitrary",), vmem_limit_bytes=VMEM_LIMIT),
        name="rglru",
    )(x3, g, w_in, conv_w, conv_b, w_ai, b_a, b_i, lam, w_out)


def _window_gates(w_a, w_i):
    wins = []
    for m, k0 in enumerate(R_WIN_K0):
        c0 = m * R_WIN_N
        halves = []
        for w in (w_a, w_i):
            acc = None
            for n in range(R_BLOCKS):
                lo, hi = max(n * R_BW, c0), min((n + 1) * R_BW, c0 + R_WIN_N)
                if lo >= hi:
                    continue
                piece = w[:, n, :, lo - n * R_BW:hi - n * R_BW]
                r0 = n * R_BW - k0
                padded = jnp.pad(piece, ((0, 0), (r0, R_WIN_K - r0 - R_BW), (lo - c0, c0 + R_WIN_N - hi)))
                acc = padded if acc is None else acc + padded
            halves.append(acc)
        wins.append(jnp.concatenate(halves, axis=-1))
    return jnp.stack(wins, axis=1).astype(BF16)


def kernel(x, ff1_norm, ff1_w_in, ff1_w_out, mix_norm, ff2_norm, ff2_w_in, ff2_w_out, m_w_in, m_b_i, m_b_f, m_head_norm, m_w_out, r_w_in, r_conv_w, r_conv_b, r_w_a, r_b_a, r_w_i, r_b_i, r_lam, r_w_out, final_norm):
    B, S, D = x.shape
    depth = ff1_norm.shape[0]
    fg = final_norm.reshape(1, D)
    ff1_g, mix_g, ff2_g = (a.reshape(depth, 1, D) for a in (ff1_norm, mix_norm, ff2_norm))
    ffn_w = [(ff1_w_in, ff1_w_out), (ff2_w_in, ff2_w_out)]
    w_cur = (ff1_w_in[0].astype(BF16), ff1_w_out[0].astype(BF16))
    m_pairs = jnp.stack([jnp.concatenate([
        m_w_in[:, :, 2 * M_DK * p:2 * M_DK * (p + 1)],
        m_w_in[:, :, M_QK + 2 * M_DK * p:M_QK + 2 * M_DK * (p + 1)],
        m_w_in[:, :, 2 * M_QK + 2 * M_DV * p:2 * M_QK + 2 * M_DV * (p + 1)],
        m_w_in[:, :, 2 * M_QK + M_V + 2 * M_DV * p:2 * M_QK + M_V + 2 * M_DV * (p + 1)],
    ], axis=-1) for p in range(M_HEADS // 2)], axis=1).astype(BF16)
    m_wgt = jnp.swapaxes(m_w_in[:, :, M_N_MAIN:], 1, 2).astype(BF16)
    m_bcol = jnp.concatenate([m_b_i, m_b_f], axis=1)[:, :, None]
    m_hn = m_head_norm[:, None, :]
    m_wo = m_w_out.astype(BF16)
    r_wi = r_w_in.astype(BF16)
    r_cw = r_conv_w.reshape(-1, CONV_W, R_WIDTH)
    r_wai = _window_gates(r_w_a, r_w_i)
    r_cb, r_ba, r_bi, r_l = (a[:, None, :] for a in (r_conv_b, r_b_a, r_b_i, r_lam))
    r_wo = r_w_out.astype(BF16)

    def ffn(xin, in_tm, out_tm, which, layer, w_cur):
        g = (ff1_g, ff2_g)[which]
        last = which == 1 and layer == depth - 1
        if last:
            return _ffn(xin, in_tm, out_tm, g, w_cur[0], w_cur[1], layer, fg, True), None
        nw = ffn_w[1 - which]
        out, nin, nout = _ffn(xin, in_tm, out_tm, g, w_cur[0], w_cur[1], layer, fg, False,
                              nxt=(nw[0], nw[1], layer + which))
        return out, (nin, nout)

    xb = x
    for layer in range(depth):
        j = layer // 2
        if layer % 2 == 0:
            xb, w_cur = ffn(xb, False, False, 0, layer, w_cur)
            xb = _mlstm(xb.reshape(B * S, D), mix_g, m_pairs, m_wgt, m_bcol, m_hn, m_wo,
                        layer, j).reshape(B, S, D)
            xb, w_cur = ffn(xb, False, False, 1, layer, w_cur)
        else:
            xt, w_cur = ffn(xb, False, True, 0, layer, w_cur)
            xt = _rglru(xt, mix_g, r_wi, r_cw, r_cb, r_wai, r_ba, r_bi, r_l, r_wo, layer, j)
            xb, w_cur = ffn(xt, True, False, 1, layer, w_cur)
    return xb
```

```python
import functools

import jax
import jax.numpy as jnp
from jax import lax
from jax.experimental import pallas as pl
from jax.experimental.pallas import tpu as pltpu

D_MODEL = 1024
BATCH = 8
EPS = 1e-6
D_FF = 2816
M_HEADS = 8
M_QK = 512
M_V = 1024
M_DK = 64
M_DV = 128
R_WIDTH = 1280
R_BLOCKS = 8
R_BW = 160
CONV_W = 4
RG_C = 8.0

LANES = 128
SUBLANES = 8
BF16_SUBLANES = 16
VMEM_LIMIT = 56 * 1024 * 1024

FFN_TM = 1024
FFN_TT = FFN_TM // BATCH
FFN_TF = 256
FFN_TR = 128
FFN_TN = 256
FFN_CONV_ROWS = 176
M_L = 256
M_TM = 1024
R_TT = 128
R_OUT_GROUP = 4
R_WIN_N = 256
R_WIN_K = 512
R_WIN_K0 = tuple(min((m * R_WIN_N // R_BW) * R_BW // LANES * LANES, R_WIDTH - R_WIN_K)
                 for m in range(R_WIDTH // R_WIN_N))
for _m, _k0 in enumerate(R_WIN_K0):
    assert _k0 <= (_m * R_WIN_N // R_BW) * R_BW
    assert _k0 + R_WIN_K >= (((_m + 1) * R_WIN_N - 1) // R_BW + 1) * R_BW
M_N_MAIN = 2 * M_QK + 2 * M_V
M_PAIR_N = 4 * M_DK + 4 * M_DV

BF16 = jnp.bfloat16
F32 = jnp.float32
F32_TINY = float(jnp.finfo(jnp.float32).tiny)

assert BATCH == SUBLANES


def _rms_scale(x, g):
    ms = jnp.mean(x * x, axis=-1, keepdims=True)
    return x * lax.rsqrt(ms + EPS) * g


def _dot(a, b):
    return jnp.dot(a, b, preferred_element_type=F32)


def _dot_nt(a, b):
    return lax.dot_general(a, b, (((1,), (1,)), ((), ())), preferred_element_type=F32)


def _softplus(x):
    return jnp.maximum(x, 0.0) + jnp.log1p(jnp.exp(-jnp.abs(x)))


def _log_sigmoid(x):
    return jnp.minimum(x, 0.0) - jnp.log1p(jnp.exp(-jnp.abs(x)))


def _layer_spec(shape, layer, ngrid):
    zeros = (0,) * len(shape)
    if ngrid == 1:
        imap = lambda i: (layer,) + zeros
    else:
        imap = lambda i, j: (layer,) + zeros
    return pl.BlockSpec((None,) + shape, imap, pipeline_mode=pl.Buffered(1))


def _ffn_kernel(x_ref, g_ref, win_ref, wout_ref, fg_ref, *rest, final_norm, swap, convert_next):
    if convert_next:
        nin_ref, nout_ref, o_ref, nin_bf_ref, nout_bf_ref, act_ref = rest
        nin_bf_ref[...] = nin_ref[...].astype(BF16)
        nout_bf_ref[...] = nout_ref[...].astype(BF16)
    else:
        o_ref, act_ref = rest
    lead, per, _ = x_ref.shape
    nb = FFN_TR // per
    g = g_ref[...]

    def put(cols, y):
        y3 = y.reshape(lead, per, y.shape[-1])
        o_ref[:, :, cols] = jnp.swapaxes(y3, 0, 1) if swap else y3

    blocks = []
    for r in range(0, FFN_TM, FFN_TR):
        x_r = x_ref[r // per:r // per + nb].reshape(FFN_TR, D_MODEL)
        xn_r = _rms_scale(x_r, g).astype(BF16)
        hg = _dot(xn_r, win_ref[:, 0:FFN_TF])
        hu = _dot(xn_r, win_ref[:, D_FF:D_FF + FFN_TF])
        act_ref[r:r + FFN_TR, 0:FFN_TF] = (hg * jax.nn.sigmoid(hg) * hu).astype(BF16)
        blocks.append(xn_r)
    xn = jnp.concatenate(blocks, axis=0)
    for c in range(1, D_FF // FFN_TF):
        lo = c * FFN_TF
        hg = _dot(xn, win_ref[:, lo:lo + FFN_TF])
        hu = _dot(xn, win_ref[:, D_FF + lo:D_FF + lo + FFN_TF])
        act_ref[:, lo:lo + FFN_TF] = (hg * jax.nn.sigmoid(hg) * hu).astype(BF16)
    if final_norm:
        y = x_ref[...].reshape(FFN_TM, D_MODEL) + 0.5 * _dot(act_ref[...], wout_ref[...])
        put(slice(0, D_MODEL), _rms_scale(y, fg_ref[...]))
    else:
        for n in range(0, D_MODEL, FFN_TN):
            cols = slice(n, n + FFN_TN)
            put(cols, x_ref[:, :, cols].reshape(FFN_TM, FFN_TN) + 0.5 * _dot(act_ref[...], wout_ref[:, cols]))


def _ffn(x3, in_tm, out_tm, g, w_in, w_out, layer, final_g, final_norm, nxt=None):
    tm_shape, bm_shape = (FFN_TT, BATCH, D_MODEL), (BATCH, FFN_TT, D_MODEL)
    tm_map, bm_map = (lambda i: (i, 0, 0)), (lambda i: (0, i, 0))
    S = x3.shape[0] if in_tm else x3.shape[1]
    steps = S // FFN_TT
    const = lambda i: (0, 0)
    in_specs = [
        pl.BlockSpec(tm_shape if in_tm else bm_shape, tm_map if in_tm else bm_map),
        _layer_spec((1, D_MODEL), layer, 1),
        pl.BlockSpec((D_MODEL, 2 * D_FF), const),
        pl.BlockSpec((D_FF, D_MODEL), const),
        pl.BlockSpec((1, D_MODEL), const),
    ]
    out_specs = [pl.BlockSpec(tm_shape if out_tm else bm_shape, tm_map if out_tm else bm_map)]
    out_shape = [jax.ShapeDtypeStruct((S, BATCH, D_MODEL) if out_tm else (BATCH, S, D_MODEL), F32)]
    args = [x3, g, w_in, w_out, final_g]
    if nxt is not None:
        nin, nout, nl = nxt
        rin = D_MODEL // steps
        nslab = D_FF // FFN_CONV_ROWS
        assert D_MODEL % steps == 0 and rin % BF16_SUBLANES == 0 and nslab <= steps
        in_map = lambda i: (nl, i, 0)
        out_map = lambda i: (nl, jnp.minimum(i, nslab - 1), 0)
        in_specs += [pl.BlockSpec((None, rin, 2 * D_FF), in_map),
                     pl.BlockSpec((None, FFN_CONV_ROWS, D_MODEL), out_map)]
        out_specs += [pl.BlockSpec((rin, 2 * D_FF), lambda i: (i, 0)),
                      pl.BlockSpec((FFN_CONV_ROWS, D_MODEL), lambda i: (jnp.minimum(i, nslab - 1), 0))]
        out_shape += [jax.ShapeDtypeStruct((D_MODEL, 2 * D_FF), BF16),
                      jax.ShapeDtypeStruct((D_FF, D_MODEL), BF16)]
        args += [nin, nout]
    res = pl.pallas_call(
        functools.partial(_ffn_kernel, final_norm=final_norm, swap=in_tm != out_tm,
                          convert_next=nxt is not None),
        grid=(steps,),
        in_specs=in_specs,
        out_specs=out_specs,
        out_shape=out_shape,
        scratch_shapes=[pltpu.VMEM((FFN_TM, D_FF), BF16)],
        compiler_params=pltpu.CompilerParams(
            dimension_semantics=("arbitrary",), vmem_limit_bytes=VMEM_LIMIT),
        name="ffn",
    )(*args)
    return res[0] if nxt is None else res


def _mlstm_kernel(x_ref, g_ref, wp_ref, wgt_ref, bcol_ref, hn_ref, wout_ref,
                  o_ref, c_ref, m_ref, y_ref):
    L = M_L
    H = M_HEADS
    PW = 2 * M_DK
    n_chunks = M_TM // L

    @pl.when(pl.program_id(1) == 0)
    def _():
        c_ref[...] = jnp.zeros_like(c_ref)
        m_ref[...] = jnp.zeros_like(m_ref)

    x = x_ref[...]
    xn = _rms_scale(x, g_ref[...]).astype(BF16)

    gr = _dot_nt(wgt_ref[...], xn) + bcol_ref[...]
    row = lax.broadcasted_iota(jnp.int32, (L, L), 0)
    col = lax.broadcasted_iota(jnp.int32, (L, L), 1)
    causal = col <= row
    upper = (row <= col).astype(BF16)
    lane = lax.broadcasted_iota(jnp.int32, (H, L), 1)

    def chunk_rows(c, m_prev):
        ig = gr[0:H, c * L:(c + 1) * L]
        lf = _log_sigmoid(gr[H:2 * H, c * L:(c + 1) * L])
        lf_hi = lf.astype(BF16)
        lf_r1 = lf - lf_hi.astype(F32)
        lf_mid = lf_r1.astype(BF16)
        lf_lo = (lf_r1 - lf_mid.astype(F32)).astype(BF16)
        b3 = _dot(jnp.concatenate([lf_hi, lf_mid, lf_lo], axis=0), upper)
        b = b3[0:H, :] + b3[H:2 * H, :] + b3[2 * H:3 * H, :]
        r = ig - b
        cmax = r
        d = 1
        while d < L:
            cmax = jnp.maximum(cmax, jnp.where(lane >= d, pltpu.roll(cmax, d, 1), -jnp.inf))
            d *= 2
        m_run = jnp.maximum(cmax, m_prev)
        w_inter = jnp.exp(m_prev - m_run)
        e_negm = jnp.exp(-(b + m_run))
        m_last = m_run[:, L - 1:L]
        cols = jnp.concatenate([m_run, w_inter, e_negm, jnp.zeros((H, L), F32)], axis=0).T
        return dict(r=r, cols=cols, decay=w_inter[:, L - 1:L], wk=jnp.exp(r - m_last),
                    m_new=b[:, L - 1:L] + m_last)

    rq = []
    m_prev = m_ref[:, 0:1]
    for c in range(n_chunks):
        rq.append(chunk_rows(c, m_prev))
        m_prev = rq[c]["m_new"]
    m_ref[...] = jnp.broadcast_to(m_prev, (H, LANES))

    pair_lane = lax.broadcasted_iota(jnp.int32, (1, PW), 1)
    ones_blk = jnp.ones((L, M_DV), BF16)
    hnorm = hn_ref[...]

    def project(p):
        z = _dot(xn, wp_ref[p])
        q_pair = (z[:, 0:PW] * (M_DK ** -0.5)).astype(BF16)
        k_f32 = z[:, PW:2 * PW]
        v_pair = z[:, 2 * PW:2 * PW + 2 * M_DV].astype(BF16)
        og_pair = z[:, 2 * PW + 2 * M_DV:2 * PW + 4 * M_DV]
        return q_pair, k_f32, v_pair, og_pair

    def heads(p, proj, c):
        ts = slice(c * L, (c + 1) * L)
        q_pair, k_f32, v_pair, og_pair = (a[ts, :] for a in proj)
        k_pair = k_f32.astype(BF16)
        kt_pair = k_f32.T
        r, cols, decay, wk = rq[c]["r"], rq[c]["cols"], rq[c]["decay"], rq[c]["wk"]
        rows = slice(p * PW, (p + 1) * PW)
        c_pair = c_ref[rows, :].astype(BF16)
        hidx = (2 * p, 2 * p + 1)
        q_m, pmat, numden = [], [], []
        for j, h in enumerate(hidx):
            in_head = (pair_lane >= j * M_DK) & (pair_lane < (j + 1) * M_DK)
            q_m.append(jnp.where(in_head, q_pair, jnp.zeros_like(q_pair)))
        for j, h in enumerate(hidx):
            dmat = jnp.exp(jnp.where(causal, r[h:h + 1, :] - cols[:, h:h + 1], -jnp.inf))
            pmat.append((_dot_nt(q_m[j], k_pair) * dmat).astype(BF16))
        for j, h in enumerate(hidx):
            v_aug = jnp.concatenate([v_pair[:, j * M_DV:(j + 1) * M_DV], ones_blk], axis=1)
            qw = (cols[:, H + h:H + h + 1] * q_m[j].astype(F32)).astype(BF16)
            numden.append(_dot(jnp.concatenate([pmat[j], qw], axis=1),
                               jnp.concatenate([v_aug, c_pair], axis=0)))
            kw = (kt_pair[j * M_DK:(j + 1) * M_DK, :] * wk[h:h + 1, :]).astype(BF16)
            hs = slice(M_DK * h, M_DK * (h + 1))
            c_ref[hs, :] = decay[h:h + 1, :] * c_ref[hs, :] + _dot(kw, v_aug)
        for j, h in enumerate(hidx):
            vs = slice(M_DV * h, M_DV * (h + 1))
            den = numden[j][:, M_DV:2 * M_DV]
            hh = numden[j][:, 0:M_DV] * (1.0 / jnp.maximum(jnp.abs(den), cols[:, 2 * H + h:2 * H + h + 1]))
            hh = hh * lax.rsqrt(jnp.mean(hh * hh, axis=1, keepdims=True) + EPS)
            hh = hh * hnorm[:, vs]
            y_ref[ts, vs] = (jax.nn.sigmoid(og_pair[:, j * M_DV:(j + 1) * M_DV]) * hh).astype(BF16)

    n_pairs = H // 2
    proj = [project(0), project(1)]
    for p in range(n_pairs):
        for c in range(n_chunks):
            heads(p, proj[p], c)
        if p + 2 < n_pairs:
            proj.append(project(p + 2))

    o_ref[...] = x + _dot(y_ref[...], wout_ref[...])


def _mlstm(x2d, g, w_pairs, wgt, bcol, hnorm, w_out, layer, j):
    nt = x2d.shape[0] // BATCH // M_TM
    return pl.pallas_call(
        _mlstm_kernel,
        grid=(BATCH, nt),
        in_specs=[
            pl.BlockSpec((M_TM, D_MODEL), lambda b, s: (b * nt + s, 0)),
            _layer_spec((1, D_MODEL), layer, 2),
            _layer_spec((M_HEADS // 2, D_MODEL, M_PAIR_N), j, 2),
            _layer_spec((2 * M_HEADS, D_MODEL), j, 2),
            _layer_spec((2 * M_HEADS, 1), j, 2),
            _layer_spec((1, M_V), j, 2),
            _layer_spec((M_V, D_MODEL), j, 2),
        ],
        out_specs=pl.BlockSpec((M_TM, D_MODEL), lambda b, s: (b * nt + s, 0)),
        out_shape=jax.ShapeDtypeStruct(x2d.shape, F32),
        scratch_shapes=[
            pltpu.VMEM((M_HEADS * M_DK, 2 * M_DV), F32),
            pltpu.VMEM((M_HEADS, LANES), F32),
            pltpu.VMEM((M_TM, M_V), BF16),
        ],
        compiler_params=pltpu.CompilerParams(
            dimension_semantics=("parallel", "arbitrary"), vmem_limit_bytes=VMEM_LIMIT),
        name="mlstm",
    )(x2d, g, w_pairs, wgt, bcol, hnorm, w_out)


def _scan_time(a, bt, h0):
    rows = lambda t: slice(t * BATCH, (t + 1) * BATCH)
    hs, hcur = [], h0
    for t in range(R_TT):
        hcur = a[rows(t), :] * hcur + bt[rows(t), :]
        hs.append(hcur)
    return jnp.concatenate(hs, axis=0), hcur


def _rglru_kernel(x_ref, g_ref, win_ref, cw_ref, cb_ref, wai_ref, ba_ref, bi_ref, lam_ref,
                  wout_ref, o_ref, ubuf_ref, ucb_ref, h_ref):
    N = R_TT * BATCH
    HIST = (CONV_W - 1) * BATCH
    W = R_WIN_N
    nwin = R_WIDTH // W

    @pl.when(pl.program_id(0) == 0)
    def _():
        ubuf_ref[0:HIST, :] = jnp.zeros((HIST, R_WIDTH), F32)
        h_ref[...] = jnp.zeros_like(h_ref)

    x = x_ref[...].reshape(N, D_MODEL)
    xn = _rms_scale(x, g_ref[...]).astype(BF16)
    neg_c = -RG_C * _softplus(-lam_ref[...])

    def in_proj(c):
        cs = slice(c * W, (c + 1) * W)
        u = _dot(xn, win_ref[:, cs])
        gate = _dot(xn, win_ref[:, R_WIDTH + c * W:R_WIDTH + (c + 1) * W])
        ubuf_ref[HIST:HIST + N, cs] = u
        uc = cb_ref[:, cs] + cw_ref[CONV_W - 1:CONV_W, cs] * u
        for j in range(CONV_W - 1):
            uc = uc + cw_ref[j:j + 1, cs] * ubuf_ref[j * BATCH:j * BATCH + N, cs]
        ubuf_ref[0:HIST, cs] = ubuf_ref[N:N + HIST, cs]
        ucb_ref[:, cs] = uc.astype(BF16)
        return uc, gate

    def gate_proj(m):
        k0 = R_WIN_K0[m]
        gw = _dot(ucb_ref[:, k0:k0 + R_WIN_K], wai_ref[m])
        return gw[:, 0:W], gw[:, W:2 * W]

    def finish(m, uc, gate, pre):
        cs = slice(m * W, (m + 1) * W)
        rg = jax.nn.sigmoid(pre[0] + ba_ref[:, cs])
        ig = jax.nn.sigmoid(pre[1] + bi_ref[:, cs])
        log_a = neg_c[:, cs] * rg
        a = jnp.exp(log_a)
        w = -jnp.tanh(log_a) * (a * a + 1.0)
        bt = (w * lax.rsqrt(jnp.maximum(w, F32_TINY))) * (ig * uc)
        h, h_last = _scan_time(a, bt, h_ref[:, cs])
        h_ref[:, cs] = h_last
        return (h * jax.nn.gelu(gate, approximate=True)).astype(BF16)

    pending = [in_proj(0), in_proj(1)]
    pre = gate_proj(0)
    acc = x
    ys = []
    for m in range(nwin):
        if m + 2 < nwin:
            pending.append(in_proj(m + 2))
        nxt = gate_proj(m + 1) if m + 1 < nwin else None
        ys.append(finish(m, pending[m][0], pending[m][1], pre))
        pre = nxt
        if len(ys) == R_OUT_GROUP or m == nwin - 1:
            lo = (m + 1 - len(ys)) * W
            acc = acc + _dot(jnp.concatenate(ys, axis=1), wout_ref[lo:(m + 1) * W, :])
            ys = []
    o_ref[...] = acc.reshape(R_TT, BATCH, D_MODEL)


def _rglru(x3, g, w_in, conv_w, conv_b, w_ai, b_a, b_i, lam, w_out, layer, j):
    S = x3.shape[0]
    n_rows = R_TT * BATCH
    return pl.pallas_call(
        _rglru_kernel,
        grid=(S // R_TT,),
        in_specs=[
            pl.BlockSpec((R_TT, BATCH, D_MODEL), lambda s: (s, 0, 0)),
            _layer_spec((1, D_MODEL), layer, 1),
            _layer_spec((D_MODEL, 2 * R_WIDTH), j, 1),
            _layer_spec((CONV_W, R_WIDTH), j, 1),
            _layer_spec((1, R_WIDTH), j, 1),
            _layer_spec((len(R_WIN_K0), R_WIN_K, 2 * R_WIN_N), j, 1),
            _layer_spec((1, R_WIDTH), j, 1),
            _layer_spec((1, R_WIDTH), j, 1),
            _layer_spec((1, R_WIDTH), j, 1),
            _layer_spec((R_WIDTH, D_MODEL), j, 1),
        ],
        out_specs=pl.BlockSpec((R_TT, BATCH, D_MODEL), lambda s: (s, 0, 0)),
        out_shape=jax.ShapeDtypeStruct(x3.shape, F32),
        scratch_shapes=[
            pltpu.VMEM((n_rows + (CONV_W - 1) * BATCH, R_WIDTH), F32),
            pltpu.VMEM((n_rows, R_WIDTH), BF16),
            pltpu.VMEM((BATCH, R_WIDTH), F32),
        ],
        compiler_params=pltpu.CompilerParams(
            dimension_semantics=("arbitrary",), vmem_limit_bytes=VMEM_LIMIT),
        name="rglru",
    )(x3, g, w_in, conv_w, conv_b, w_ai, b_a, b_i, lam, w_out)


def _window_gates(w_a, w_i):
    wins = []
    for m, k0 in enumerate(R_WIN_K0):
        c0 = m * R_WIN_N
        halves = []
        for w in (w_a, w_i):
            acc = None
            for n in range(R_BLOCKS):
                lo, hi = max(n * R_BW, c0), min((n + 1) * R_BW, c0 + R_WIN_N)
                if lo >= hi:
                    continue
                piece = w[:, n, :, lo - n * R_BW:hi - n * R_BW]
                r0 = n * R_BW - k0
                padded = jnp.pad(piece, ((0, 0), (r0, R_WIN_K - r0 - R_BW), (lo - c0, c0 + R_WIN_N - hi)))
                acc = padded if acc is None else acc + padded
            halves.append(acc)
        wins.append(jnp.concatenate(halves, axis=-1))
    return jnp.stack(wins, axis=1).astype(BF16)


def kernel(x, ff1_norm, ff1_w_in, ff1_w_out, mix_norm, ff2_norm, ff2_w_in, ff2_w_out, m_w_in, m_b_i, m_b_f, m_head_norm, m_w_out, r_w_in, r_conv_w, r_conv_b, r_w_a, r_b_a, r_w_i, r_b_i, r_lam, r_w_out, final_norm):
    B, S, D = x.shape
    depth = ff1_norm.shape[0]
    fg = final_norm.reshape(1, D)
    ff1_g, mix_g, ff2_g = (a.reshape(depth, 1, D) for a in (ff1_norm, mix_norm, ff2_norm))
    ffn_w = [(ff1_w_in, ff1_w_out), (ff2_w_in, ff2_w_out)]
    w_cur = (ff1_w_in[0].astype(BF16), ff1_w_out[0].astype(BF16))
    m_pairs = jnp.stack([jnp.concatenate([
        m_w_in[:, :, 2 * M_DK * p:2 * M_DK * (p + 1)],
        m_w_in[:, :, M_QK + 2 * M_DK * p:M_QK + 2 * M_DK * (p + 1)],
        m_w_in[:, :, 2 * M_QK + 2 * M_DV * p:2 * M_QK + 2 * M_DV * (p + 1)],
        m_w_in[:, :, 2 * M_QK + M_V + 2 * M_DV * p:2 * M_QK + M_V + 2 * M_DV * (p + 1)],
    ], axis=-1) for p in range(M_HEADS // 2)], axis=1).astype(BF16)
    m_wgt = jnp.swapaxes(m_w_in[:, :, M_N_MAIN:], 1, 2).astype(BF16)
    m_bcol = jnp.concatenate([m_b_i, m_b_f], axis=1)[:, :, None]
    m_hn = m_head_norm[:, None, :]
    m_wo = m_w_out.astype(BF16)
    r_wi = r_w_in.astype(BF16)
    r_cw = r_conv_w.reshape(-1, CONV_W, R_WIDTH)
    r_wai = _window_gates(r_w_a, r_w_i)
    r_cb, r_ba, r_bi, r_l = (a[:, None, :] for a in (r_conv_b, r_b_a, r_b_i, r_lam))
    r_wo = r_w_out.astype(BF16)

    def ffn(xin, in_tm, out_tm, which, layer, w_cur):
        g = (ff1_g, ff2_g)[which]
        last = which == 1 and layer == depth - 1
        if last:
            return _ffn(xin, in_tm, out_tm, g, w_cur[0], w_cur[1], layer, fg, True), None
        nw = ffn_w[1 - which]
        out, nin, nout = _ffn(xin, in_tm, out_tm, g, w_cur[0], w_cur[1], layer, fg, False,
                              nxt=(nw[0], nw[1], layer + which))
        return out, (nin, nout)

    xb = x
    for layer in range(depth):
        j = layer // 2
        if layer % 2 == 0:
            xb, w_cur = ffn(xb, False, False, 0, layer, w_cur)
            xb = _mlstm(xb.reshape(B * S, D), mix_g, m_pairs, m_wgt, m_bcol, m_hn, m_wo,
                        layer, j).reshape(B, S, D)
            xb, w_cur = ffn(xb, False, False, 1, layer, w_cur)
        else:
            xt, w_cur = ffn(xb, False, True, 0, layer, w_cur)
            xt = _rglru(xt, mix_g, r_wi, r_cw, r_cb, r_wai, r_ba, r_bi, r_l, r_wo, layer, j)
            xb, w_cur = ffn(xt, True, False, 1, layer, w_cur)
    return xb
```
